```python
import math, functools
import jax, jax.numpy as jnp
from jax import lax
import numpy as np

D_MODEL = 2048
BATCH = 2
SEQ = 4096
DEPTH = 2
DEC_BATCH = 128
DEC_SEQ = 8
PAST_LEN = 2048
PAGE_SIZE = 128

ATT_HEADS = 8
ATT_DIM = 128
ATT_W = ATT_HEADS * ATT_DIM
MOBA_BLOCK = 256
MOBA_TOPK = 3
MOBA_QCHUNK = 64
RET_HEADS = 4
RET_DK = 256
RET_DV = 256
RET_QK_W = RET_HEADS * RET_DK
RET_VW = RET_HEADS * RET_DV
RET_CHUNK = 128
CONV_CH = 1024
CONV_WIDTH = 31
D_FF = 5632
FFN_CONV_WIDTH = 3
BRANCH_W = ATT_W
N_BRANCH = 3
LN_EPS = 1e-5
ALPHA = (2 * DEPTH) ** 0.25
BETA = (8 * DEPTH) ** -0.25
IN_SPLITS = (ATT_W, ATT_W, ATT_W, RET_QK_W, RET_QK_W, RET_VW, RET_VW, 2 * CONV_CH, D_MODEL, D_MODEL, D_MODEL)
IN_SCALES = (1.0, 1.0, BETA, 1.0, 1.0, BETA, 1.0, 1.0, 1.0, 1.0, 1.0)
IN_WIDTH = sum(IN_SPLITS)
IN_OFFSETS = tuple(int(v) for v in np.cumsum(IN_SPLITS)[:-1])

kernel_name = 'hybrid_moba_retention_conformer_step'


def layer_norm(x, g, b):
    xf = x.astype(jnp.float32)
    mu = jnp.mean(xf, -1, keepdims=True)
    var = jnp.mean(jnp.square(xf - mu), -1, keepdims=True)
    return ((xf - mu) * lax.rsqrt(var + LN_EPS)).astype(x.dtype) * g + b


def causal_dwconv(buf, u, w, b):
    xp = jnp.concatenate([buf.astype(u.dtype), u], axis=1)
    y = lax.conv_general_dilated(xp, w[:, None, :].astype(u.dtype), (1,), 'VALID',
                                 dimension_numbers=('NWC', 'WIO', 'NWC'),
                                 feature_group_count=u.shape[-1])
    return y + b, xp[:, xp.shape[1] - (w.shape[0] - 1):]


def retnet_rotate(x, pos):
    inv = 1.0 / (10000.0 ** jnp.linspace(0.0, 1.0, x.shape[-1] // 2, dtype=jnp.float32))
    ang = pos.astype(jnp.float32)[:, None] * inv[None, :]
    cos = jnp.cos(ang)[None, :, None, :].astype(x.dtype)
    sin = jnp.sin(ang)[None, :, None, :].astype(x.dtype)
    x1, x2 = x[..., 0::2], x[..., 1::2]
    return jnp.stack([x1 * cos - x2 * sin, x2 * cos + x1 * sin], axis=-1).reshape(x.shape)


def retention(q, k, v, r0):
    B, T, H, _ = q.shape
    C = math.gcd(RET_CHUNK, T)
    n = T // C
    log_g = jnp.log1p(-jnp.exp2(-5.0 - jnp.arange(H, dtype=jnp.float32)))
    i = jnp.arange(C, dtype=jnp.float32)
    diff = i[:, None] - i[None, :]
    intra = jnp.where(diff >= 0, jnp.exp(log_g[:, None, None] * jnp.maximum(diff, 0.0)), 0.0)
    q_decay = jnp.exp(log_g[None, :] * (i[:, None] + 1.0))
    k_decay = jnp.exp(log_g[None, :] * (C - 1.0 - i[:, None]))
    chunk_decay = jnp.exp(log_g * C)

    def to_chunks(a):
        return a.reshape(B, n, C, H, a.shape[-1]).transpose(1, 0, 2, 3, 4)

    def step(r, qkv):
        qc, kc, vc = qkv
        s = jnp.einsum('bihd,bjhd->bhij', qc, kc) * intra[None]
        o = (jnp.einsum('bhij,bjhe->bihe', s, vc)
             + jnp.einsum('bihd,bhde->bihe', qc, r) * q_decay[None, :, :, None])
        r = (r * chunk_decay[None, :, None, None]
             + jnp.einsum('bjhd,bjhe->bhde', kc * k_decay[None, :, :, None], vc))
        return r, o

    r, o = lax.scan(step, r0, (to_chunks(q), to_chunks(k), to_chunks(v)))
    return o.transpose(1, 0, 2, 3, 4).reshape(B, T, H, -1), r


def to_blocks(a):
    Tk, H, d = a.shape
    nb = -(-Tk // MOBA_BLOCK)
    a = jnp.pad(a, ((0, nb * MOBA_BLOCK - Tk), (0, 0), (0, 0)))
    return a.reshape(nb, MOBA_BLOCK, H, d).transpose(2, 0, 1, 3)


def moba_core(q, qpos, kblk, vblk, kmean):
    Tq, H = q.shape[0], q.shape[1]
    nb = kblk.shape[1]
    n_sel = min(MOBA_TOPK, nb)
    qblk = qpos // MOBA_BLOCK
    score = jnp.einsum('qhd,hnd->qhn', q.astype(jnp.float32), kmean)
    past = jnp.arange(nb, dtype=jnp.int32)[None, None, :] < qblk[:, None, None]
    score = jnp.where(past, score, -jnp.inf)
    _, sel = lax.top_k(score, n_sel)
    valid = sel < qblk[:, None, None]
    blocks = jnp.concatenate([sel, jnp.broadcast_to(qblk[:, None, None], (Tq, H, 1))], axis=-1)
    slot_ok = jnp.concatenate([valid, jnp.ones((Tq, H, 1), dtype=bool)], axis=-1)
    heads = jnp.arange(H)[None, :, None]
    kg = kblk[heads, blocks]
    vg = vblk[heads, blocks]
    s = jnp.einsum('qhd,qhsjd->qhsj', q, kg).astype(jnp.float32) * (ATT_DIM ** -0.5)
    kpos = blocks[..., None] * MOBA_BLOCK + jnp.arange(MOBA_BLOCK, dtype=jnp.int32)
    mask = slot_ok[..., None] & (kpos <= qpos[:, None, None, None])
    s = jnp.where(mask, s, -1e30)
    p = jax.nn.softmax(s.reshape(Tq, H, -1), axis=-1).reshape(s.shape).astype(vg.dtype)
    return jnp.einsum('qhsj,qhsjd->qhd', p, vg)


def moba_prompt(q, k, v):
    B, T, H, d = q.shape
    kblk = jax.vmap(to_blocks)(k)
    vblk = jax.vmap(to_blocks)(v)
    kmean = jnp.mean(kblk.astype(jnp.float32), axis=3)
    qc = math.gcd(MOBA_QCHUNK, T)
    nc = T // qc
    qs = q.reshape(B * nc, qc, H, d)
    bidx = jnp.repeat(jnp.arange(B, dtype=jnp.int32), nc)
    qpos = jnp.tile(jnp.arange(T, dtype=jnp.int32).reshape(nc, qc), (B, 1))

    def one(a):
        b, qq, pp = a
        return moba_core(qq, pp, kblk[b], vblk[b], kmean[b])

    return lax.map(one, (bidx, qs, qpos)).reshape(B, T, H, d)


def moba_sample(cache_k_l, cache_v_l, page_table, q, k, v):
    DB, S, H, d = q.shape
    P = page_table.shape[1] * PAGE_SIZE
    qpos = P + jnp.arange(S, dtype=jnp.int32)

    def one(a):
        pt, qq, kk, vv = a
        kf = jnp.concatenate([cache_k_l[pt].reshape(P, H, d).astype(kk.dtype), kk], axis=0)
        vf = jnp.concatenate([cache_v_l[pt].reshape(P, H, d).astype(vv.dtype), vv], axis=0)
        kblk = to_blocks(kf)
        vblk = to_blocks(vf)
        return moba_core(qq, qpos, kblk, vblk, jnp.mean(kblk.astype(jnp.float32), axis=2))

    return lax.map(one, (page_table, q, k, v))


def trunk_layer(x, c, attend, conv_buf, ret_state, ffn_buf, pos0,
                w_ada, b_ada, w_in, w_branch, w_o, ln1_g, ln1_b, ret_gn_g,
                conv_w, conv_b, conv_ln_g, conv_ln_b, w_up, ffn_conv_w, ffn_conv_b,
                w_down, ln2_g, ln2_b):
    B, T, _ = x.shape
    ada = jax.nn.silu(c) @ w_ada + b_ada
    sh1, sc1, g1, sh2, sc2, g2 = jnp.split(ada[:, None, :], 6, axis=-1)
    h = x * (1.0 + sc1) + sh1
    z = h @ w_in
    qa, ka, va, qb, kb, vb, gb, zc, gate_a, gate_b, gate_c = jnp.split(z, IN_OFFSETS, axis=-1)
    ka = ka.reshape(B, T, ATT_HEADS, ATT_DIM)
    va = va.reshape(B, T, ATT_HEADS, ATT_DIM)
    ya = attend(qa.reshape(B, T, ATT_HEADS, ATT_DIM), ka, va).reshape(B, T, ATT_W)
    pos = pos0 + jnp.arange(T, dtype=jnp.int32)
    qr = retnet_rotate(qb.reshape(B, T, RET_HEADS, RET_DK), pos).astype(jnp.float32)
    kr = (retnet_rotate(kb.reshape(B, T, RET_HEADS, RET_DK), pos) * (RET_DK ** -0.5)).astype(jnp.float32)
    o, ret_new = retention(qr, kr, vb.reshape(B, T, RET_HEADS, RET_DV).astype(jnp.float32),
                           ret_state.astype(jnp.float32))
    mu = jnp.mean(o, -1, keepdims=True)
    var = jnp.mean(jnp.square(o - mu), -1, keepdims=True)
    o = ((o - mu) * lax.rsqrt(var + LN_EPS)).reshape(B, T, RET_VW).astype(x.dtype) * ret_gn_g
    yb = jax.nn.silu(gb) * o
    glu = zc[..., :CONV_CH] * jax.nn.sigmoid(zc[..., CONV_CH:])
    yc, conv_new = causal_dwconv(conv_buf, glu, conv_w, conv_b)
    yc = jax.nn.silu(layer_norm(yc, conv_ln_g, conv_ln_b))
    merged = (jax.nn.sigmoid(gate_a) * (ya @ w_branch[0])
              + jax.nn.sigmoid(gate_b) * (yb @ w_branch[1])
              + jax.nn.sigmoid(gate_c) * (yc @ w_branch[2]))
    x = layer_norm(ALPHA * x + (1.0 + g1) * (merged @ w_o), ln1_g, ln1_b)
    h = x * (1.0 + sc2) + sh2
    gate_in, up = jnp.split(h @ w_up, 2, axis=-1)
    gconv, ffn_new = causal_dwconv(ffn_buf, gate_in, ffn_conv_w, ffn_conv_b)
    x = layer_norm(ALPHA * x + (1.0 + g2) * ((jax.nn.silu(gconv) * up) @ w_down), ln2_g, ln2_b)
    return x, ka, va, conv_new, ret_new.astype(ret_state.dtype), ffn_new


def setup_inputs(seed: int = 0) -> dict:
    key = jax.random.key(seed)
    ks = jax.random.split(key, 28)
    f32 = jnp.float32
    n_pages = PAST_LEN // PAGE_SIZE
    n_used = DEC_BATCH * n_pages
    n_phys = n_used + max(1, n_used // 4)

    def nrm(k, shape, s):
        return jax.random.normal(k, shape, f32) * s

    col_scale = jnp.concatenate([jnp.full((w,), s, f32) for w, s in zip(IN_SPLITS, IN_SCALES)])
    page_table = jax.random.permutation(ks[7], n_phys)[:n_used].reshape(DEC_BATCH, n_pages).astype(jnp.int32)
    return {
        'x_prompt': nrm(ks[0], (BATCH, SEQ, D_MODEL), 1.0),
        'x_sample': nrm(ks[1], (DEC_BATCH, DEC_SEQ, D_MODEL), 1.0),
        'cache_k': nrm(ks[2], (DEPTH, n_phys, PAGE_SIZE, ATT_HEADS, ATT_DIM), 1.0),
        'cache_v': nrm(ks[3], (DEPTH, n_phys, PAGE_SIZE, ATT_HEADS, ATT_DIM), 1.0),
        'state_conv': nrm(ks[4], (DEPTH, DEC_BATCH, CONV_WIDTH - 1, CONV_CH), 0.5),
        'state_ret': nrm(ks[5], (DEPTH, DEC_BATCH, RET_HEADS, RET_DK, RET_DV), 0.5),
        'state_ffn': nrm(ks[6], (DEPTH, DEC_BATCH, FFN_CONV_WIDTH - 1, D_FF), 0.5),
        'page_table': page_table,
        'c_prompt': nrm(ks[8], (BATCH, D_MODEL), 1.0),
        'c_sample': nrm(ks[9], (DEC_BATCH, D_MODEL), 1.0),
        'w_ada': nrm(ks[10], (DEPTH, D_MODEL, 6 * D_MODEL), 0.1 * D_MODEL ** -0.5),
        'b_ada': nrm(ks[11], (DEPTH, 6 * D_MODEL), 0.01),
        'w_in': nrm(ks[12], (DEPTH, D_MODEL, IN_WIDTH), D_MODEL ** -0.5) * col_scale,
        'w_branch': nrm(ks[13], (DEPTH, N_BRANCH, BRANCH_W, D_MODEL), BETA * BRANCH_W ** -0.5),
        'w_o': nrm(ks[14], (DEPTH, D_MODEL, D_MODEL), BETA * D_MODEL ** -0.5),
        'ln1_g': 1.0 + nrm(ks[15], (DEPTH, D_MODEL), 0.02),
        'ln1_b': nrm(ks[16], (DEPTH, D_MODEL), 0.02),
        'ret_gn_g': 1.0 + nrm(ks[17], (DEPTH, RET_VW), 0.02),
        'conv_w': nrm(ks[18], (DEPTH, CONV_WIDTH, CONV_CH), CONV_WIDTH ** -0.5),
        'conv_b': nrm(ks[19], (DEPTH, CONV_CH), 0.02),
        'conv_ln_g': 1.0 + nrm(ks[20], (DEPTH, CONV_CH), 0.02),
        'conv_ln_b': nrm(ks[21], (DEPTH, CONV_CH), 0.02),
        'w_up': nrm(ks[22], (DEPTH, D_MODEL, 2 * D_FF), BETA * D_MODEL ** -0.5),
        'ffn_conv_w': nrm(ks[23], (DEPTH, FFN_CONV_WIDTH, D_FF), FFN_CONV_WIDTH ** -0.5),
        'ffn_conv_b': nrm(ks[24], (DEPTH, D_FF), 0.02),
        'w_down': nrm(ks[25], (DEPTH, D_FF, D_MODEL), BETA * D_FF ** -0.5),
        'ln2_g': 1.0 + nrm(ks[26], (DEPTH, D_MODEL), 0.02),
        'ln2_b': nrm(ks[27], (DEPTH, D_MODEL), 0.02),
    }


def reference(x_prompt, x_sample, cache_k, cache_v, state_conv, state_ret, state_ffn, page_table,
              c_prompt, c_sample, w_ada, b_ada, w_in, w_branch, w_o, ln1_g, ln1_b, ret_gn_g,
              conv_w, conv_b, conv_ln_g, conv_ln_b, w_up, ffn_conv_w, ffn_conv_b, w_down, ln2_g, ln2_b):
    B = x_prompt.shape[0]
    dt = x_prompt.dtype
    P = page_table.shape[1] * PAGE_SIZE
    xp, xs = x_prompt, x_sample
    pk, pv, pc, pr, pf = [], [], [], [], []
    sk, sv, sc, sr, sf = [], [], [], [], []
    for l in range(DEPTH):
        params = (w_ada[l], b_ada[l], w_in[l], w_branch[l], w_o[l], ln1_g[l], ln1_b[l], ret_gn_g[l],
                  conv_w[l], conv_b[l], conv_ln_g[l], conv_ln_b[l], w_up[l], ffn_conv_w[l], ffn_conv_b[l],
                  w_down[l], ln2_g[l], ln2_b[l])
        xp, k1, v1, c1, r1, f1 = trunk_layer(
            xp, c_prompt, moba_prompt,
            jnp.zeros((B, CONV_WIDTH - 1, CONV_CH), dt),
            jnp.zeros((B, RET_HEADS, RET_DK, RET_DV), state_ret.dtype),
            jnp.zeros((B, FFN_CONV_WIDTH - 1, D_FF), dt), 0, *params)
        xs, k2, v2, c2, r2, f2 = trunk_layer(
            xs, c_sample, functools.partial(moba_sample, cache_k[l], cache_v[l], page_table),
            state_conv[l], state_ret[l], state_ffn[l], P, *params)
        pk.append(k1); pv.append(v1); pc.append(c1); pr.append(r1); pf.append(f1)
        sk.append(k2); sv.append(v2); sc.append(c2); sr.append(r2); sf.append(f2)
    return (xp, xs, jnp.stack(pk), jnp.stack(pv), jnp.stack(pc), jnp.stack(pr), jnp.stack(pf),
            jnp.stack(sk), jnp.stack(sv), jnp.stack(sc), jnp.stack(sr), jnp.stack(sf))
```

```python
import functools
import math

import jax
import jax.numpy as jnp
from jax import lax
from jax.experimental import pallas as pl
from jax.experimental.pallas import tpu as pltpu

F32 = jnp.float32
BF16 = jnp.bfloat16

PAGE_SIZE = 128
ATT_HEADS = 8
ATT_DIM = 128
ATT_W = ATT_HEADS * ATT_DIM
MOBA_BLOCK = 256
MOBA_TOPK = 3
RET_HEADS = 4
RET_DK = 256
RET_DV = 256
RET_CHUNK = 128
CONV_CH = 1024
CONV_WIDTH = 31
FFN_CONV_WIDTH = 3
LN_EPS = 1e-5
NEG_BIG = -1e30

OFF_AQ, OFF_AK, OFF_AV = 0, ATT_W, 2 * ATT_W
OFF_BQ = 3 * ATT_W
OFF_BK = OFF_BQ + RET_HEADS * RET_DK
OFF_BV = OFF_BK + RET_HEADS * RET_DK
OFF_BG = OFF_BV + RET_HEADS * RET_DV
OFF_C = OFF_BG + RET_HEADS * RET_DV
OFF_GA = OFF_C + 2 * CONV_CH

VMEM_LIMIT_BYTES = 56 * 2**20
HIST_ROWS = 32
FFN_HIST_ROWS = 8


def _params(*sem):
    return pltpu.CompilerParams(dimension_semantics=sem, vmem_limit_bytes=VMEM_LIMIT_BYTES)


def _dot(a, b):
    return jnp.dot(a, b, preferred_element_type=F32)


def _dot_nt(a, b, precision=None):
    return lax.dot_general(a, b, (((1,), (1,)), ((), ())), precision=precision,
                           preferred_element_type=F32)


def _dot_tn(a, b):
    return lax.dot_general(a, b, (((0,), (0,)), ((), ())), preferred_element_type=F32)


def _silu(x):
    return x * jax.nn.sigmoid(x)


def _layer_norm(x, g, b):
    mu = jnp.mean(x, -1, keepdims=True)
    xc = x - mu
    var = jnp.mean(xc * xc, -1, keepdims=True)
    return xc * lax.rsqrt(var + LN_EPS) * g + b


def _ada_body(c_ref, w_ref, b_ref, o_ref):
    a = _silu(c_ref[...]).astype(BF16)
    o_ref[...] = _dot(a, w_ref[...].astype(BF16)) + b_ref[...]


def ada_all(c_all, w_ada, b_ada, tn=1024):
    depth, d, n = w_ada.shape
    r = c_all.shape[0]
    return pl.pallas_call(
        _ada_body,
        grid=(depth, n // tn),
        in_specs=[pl.BlockSpec((r, d), lambda l, j: (0, 0)),
                  pl.BlockSpec((None, d, tn), lambda l, j: (l, 0, j)),
                  pl.BlockSpec((None, 1, tn), lambda l, j: (l, 0, j))],
        out_specs=pl.BlockSpec((None, r, tn), lambda l, j: (l, 0, j)),
        out_shape=jax.ShapeDtypeStruct((depth, r, n), F32),
        compiler_params=_params("arbitrary", "arbitrary"),
        name="ada",
    )(c_all, w_ada, b_ada.reshape(depth, 1, n))


def _modmm_body(x_ref, sc_ref, sh_ref, w_ref, o_ref, h_ref):
    @pl.when(pl.program_id(2) == 0)
    def _():
        h = x_ref[...] * (1.0 + sc_ref[...]) + sh_ref[...]
        h_ref[...] = h.reshape(h_ref.shape).astype(BF16)

    o_ref[...] = _dot(h_ref[...], w_ref[...]).reshape(o_ref.shape)


def modulated_matmul(x, ada, sc_chunk, sh_chunk, w, nb, t, tn):
    s, tt, d = x.shape
    n = w.shape[1]
    return pl.pallas_call(
        _modmm_body,
        grid=(s // nb, tt // t, n // tn),
        in_specs=[pl.BlockSpec((nb, t, d), lambda i, r, j: (i, r, 0)),
                  pl.BlockSpec((nb, 1, d), lambda i, r, j: (i, 0, sc_chunk)),
                  pl.BlockSpec((nb, 1, d), lambda i, r, j: (i, 0, sh_chunk)),
                  pl.BlockSpec((d, tn), lambda i, r, j: (0, j))],
        out_specs=pl.BlockSpec((nb, t, tn), lambda i, r, j: (i, r, j)),
        out_shape=jax.ShapeDtypeStruct((s, tt, n), F32),
        scratch_shapes=[pltpu.VMEM((nb * t, d), BF16)],
        compiler_params=_params("arbitrary", "arbitrary", "arbitrary"),
        name="modmm",
    )(x, ada, ada, w)


def _select_blocks(score, n_past, n_blocks):
    lane = lax.broadcasted_iota(jnp.int32, score.shape, 1)
    past = lane < n_past
    s = jnp.where(past, score, -jnp.inf)
    rank = jnp.zeros(score.shape, jnp.int32)
    for m in range(n_blocks):
        sm = s[:, m:m + 1]
        beats = jnp.where(sm > s, 1, jnp.where(sm == s, jnp.where(lane > m, 1, 0), 0))
        rank = rank + beats
    return jnp.where(past, jnp.where(rank < MOBA_TOPK, 1.0, 0.0), 0.0)


def _moba_prompt_body(q_ref, k_ref, v_ref, o_ref, kb_ref, vb_ref, km_ref):
    qi = pl.program_id(2)
    n_blocks = k_ref.shape[0] // MOBA_BLOCK
    scale = ATT_DIM ** -0.5

    @pl.when(qi == 0)
    def _():
        kb_ref[...] = k_ref[...].astype(BF16)
        vb_ref[...] = v_ref[...].astype(BF16)
        km_ref[...] = jnp.zeros_like(km_ref)
        for n in range(n_blocks):
            km_ref[n:n + 1, :] = jnp.mean(k_ref[n * MOBA_BLOCK:(n + 1) * MOBA_BLOCK, :],
                                          axis=0, keepdims=True)

    q = q_ref[...]
    qb = q.astype(BF16)
    score = _dot_nt(q, km_ref[...], precision=lax.Precision.HIGHEST)
    sel = _select_blocks(score, qi, n_blocks)
    lane = lax.broadcasted_iota(jnp.int32, sel.shape, 1)

    start = pl.multiple_of(qi * MOBA_BLOCK, MOBA_BLOCK)
    row = lax.broadcasted_iota(jnp.int32, (MOBA_BLOCK, MOBA_BLOCK), 0)
    col = lax.broadcasted_iota(jnp.int32, (MOBA_BLOCK, MOBA_BLOCK), 1)
    s0 = _dot_nt(qb, kb_ref[pl.ds(start, MOBA_BLOCK), :]) * scale
    s0 = jnp.where(col <= row, s0, NEG_BIG)
    m0 = jnp.max(s0, axis=1, keepdims=True)
    p0 = jnp.exp(s0 - m0)
    l0 = jnp.sum(p0, axis=1, keepdims=True)
    acc0 = _dot(p0.astype(BF16), vb_ref[pl.ds(start, MOBA_BLOCK), :])

    def body(j, carry):
        m, l, acc = carry
        off = pl.multiple_of(j * MOBA_BLOCK, MOBA_BLOCK)
        sj = _dot_nt(qb, kb_ref[pl.ds(off, MOBA_BLOCK), :]) * scale
        selj = jnp.sum(jnp.where(lane == j, sel, 0.0), axis=1, keepdims=True)
        sj = jnp.where(selj > 0.0, sj, NEG_BIG)
        m_new = jnp.maximum(m, jnp.max(sj, axis=1, keepdims=True))
        alpha = jnp.exp(m - m_new)
        p = jnp.exp(sj - m_new)
        l = alpha * l + jnp.sum(p, axis=1, keepdims=True)
        acc = alpha * acc + _dot(p.astype(BF16), vb_ref[pl.ds(off, MOBA_BLOCK), :])
        return m_new, l, acc

    _, l, acc = lax.fori_loop(0, qi, body, (m0, l0, acc0))
    o_ref[...] = (acc / l).astype(o_ref.dtype)


def moba_prompt(z):
    b, t, _ = z.shape
    assert t % MOBA_BLOCK == 0
    nq = t // MOBA_BLOCK
    kcol = OFF_AK // ATT_DIM
    vcol = OFF_AV // ATT_DIM
    return pl.pallas_call(
        _moba_prompt_body,
        grid=(b, ATT_HEADS, nq),
        in_specs=[pl.BlockSpec((None, MOBA_BLOCK, ATT_DIM), lambda i, h, q: (i, q, h)),
                  pl.BlockSpec((None, t, ATT_DIM), lambda i, h, q: (i, 0, kcol + h)),
                  pl.BlockSpec((None, t, ATT_DIM), lambda i, h, q: (i, 0, vcol + h))],
        out_specs=pl.BlockSpec((None, MOBA_BLOCK, ATT_DIM), lambda i, h, q: (i, q, h)),
        out_shape=jax.ShapeDtypeStruct((b, t, ATT_W), BF16),
        scratch_shapes=[pltpu.VMEM((t, ATT_DIM), BF16), pltpu.VMEM((t, ATT_DIM), BF16),
                        pltpu.VMEM((128, ATT_DIM), F32)],
        compiler_params=_params("arbitrary", "arbitrary", "arbitrary"),
        name="moba_prompt",
    )(z, z, z)


def _attend_partial(qbd_b, kb, vb, mask):
    s = _dot_nt(qbd_b, kb) * (ATT_DIM ** -0.5)
    if mask is not None:
        s = jnp.where(mask, s, NEG_BIG)
    m = jnp.max(s, axis=1, keepdims=True)
    p = jnp.exp(s - m)
    l = jnp.sum(p, axis=1, keepdims=True)
    full = _dot(p.astype(BF16), vb)
    rows = full.shape[0] // ATT_HEADS
    acc = jnp.concatenate(
        [full[h * rows:(h + 1) * rows, h * ATT_DIM:(h + 1) * ATT_DIM] for h in range(ATT_HEADS)],
        axis=0)
    return m, l, acc


def _moba_sample_body(pt_ref, q_ref, kn_ref, vn_ref, k0_ref, k1_ref, v0_ref, v1_ref, o_ref,
                      qbd_ref, m_ref, l_ref, acc_ref, ks_ref, km_ref, ko_ref, vo_ref):
    del pt_ref
    n = pl.program_id(1)
    n_past = pl.num_programs(1)
    rows = q_ref.shape[0]
    hr = ATT_HEADS * rows

    @pl.when(n == 0)
    def _():
        q = q_ref[...]
        head = lax.broadcasted_iota(jnp.int32, q.shape, 1) // ATT_DIM
        qbd_ref[...] = jnp.concatenate(
            [jnp.where(head == h, q, 0.0) for h in range(ATT_HEADS)], axis=0)
        ko_ref[...] = jnp.zeros_like(ko_ref)
        vo_ref[...] = jnp.zeros_like(vo_ref)
        ko_ref[0:rows, :] = kn_ref[...].astype(BF16)
        vo_ref[0:rows, :] = vn_ref[...].astype(BF16)

    qbd_b = qbd_ref[...].astype(BF16)
    kblk = jnp.concatenate([k0_ref[...], k1_ref[...]], axis=0)
    vblk = jnp.concatenate([v0_ref[...], v1_ref[...]], axis=0)
    m, l, acc = _attend_partial(qbd_b, kblk.astype(BF16), vblk.astype(BF16), None)
    m_ref[n] = jnp.broadcast_to(m, (hr, 128))
    l_ref[n] = jnp.broadcast_to(l, (hr, 128))
    acc_ref[n] = acc
    ks_ref[n] = jnp.sum(kblk.reshape(MOBA_BLOCK // 8, 8, ATT_W), axis=0)

    @pl.when(n == n_past - 1)
    def _():
        nb_static = m_ref.shape[0]
        km_ref[...] = jnp.zeros_like(km_ref)
        for j in range(nb_static):
            km_ref[j:j + 1, :] = jnp.sum(ks_ref[j], axis=0, keepdims=True) * (1.0 / MOBA_BLOCK)
        score = _dot_nt(qbd_ref[...], km_ref[...], precision=lax.Precision.HIGHEST)
        sel = _select_blocks(score, nb_static, nb_static)
        qrow = lax.broadcasted_iota(jnp.int32, (hr, 128), 0) % rows
        kcol = lax.broadcasted_iota(jnp.int32, (hr, 128), 1)
        m_o, l_o, acc_o = _attend_partial(qbd_b, ko_ref[...], vo_ref[...], kcol <= qrow)
        m_tot = m_o
        for j in range(nb_static):
            m_tot = jnp.maximum(m_tot, jnp.where(sel[:, j:j + 1] > 0.0, m_ref[j][:, 0:1], NEG_BIG))
        w_o = jnp.exp(m_o - m_tot)
        num = w_o * acc_o
        den = w_o * l_o
        for j in range(nb_static):
            w = jnp.where(sel[:, j:j + 1] > 0.0, jnp.exp(m_ref[j][:, 0:1] - m_tot), 0.0)
            num = num + w * acc_ref[j]
            den = den + w * l_ref[j][:, 0:1]
        out = num / den
        for h in range(ATT_HEADS):
            o_ref[:, h * ATT_DIM:(h + 1) * ATT_DIM] = out[h * rows:(h + 1) * rows, :].astype(o_ref.dtype)


def moba_sample(z, cache_k, cache_v, page_table, layer):
    s, rows, _ = z.shape
    n_pages = page_table.shape[1]
    pages_per_block = MOBA_BLOCK // PAGE_SIZE
    assert pages_per_block == 2 and n_pages % pages_per_block == 0 and rows <= 128
    n_past = n_pages // pages_per_block
    assert n_past <= 128
    depth, n_phys = cache_k.shape[0], cache_k.shape[1]
    ck = cache_k.reshape(depth, n_phys, PAGE_SIZE, ATT_W)
    cv = cache_v.reshape(depth, n_phys, PAGE_SIZE, ATT_W)
    hr = ATT_HEADS * rows

    def page_spec(which):
        return pl.BlockSpec((None, None, PAGE_SIZE, ATT_W),
                            lambda i, n, pt: (layer, pt[i, pages_per_block * n + which], 0, 0))

    def z_spec(col):
        return pl.BlockSpec((None, rows, ATT_W), lambda i, n, pt: (i, 0, col))

    grid_spec = pltpu.PrefetchScalarGridSpec(
        num_scalar_prefetch=1,
        grid=(s, n_past),
        in_specs=[z_spec(OFF_AQ // ATT_W), z_spec(OFF_AK // ATT_W), z_spec(OFF_AV // ATT_W),
                  page_spec(0), page_spec(1), page_spec(0), page_spec(1)],
        out_specs=pl.BlockSpec((None, rows, ATT_W), lambda i, n, pt: (i, 0, 0)),
        scratch_shapes=[pltpu.VMEM((hr, ATT_W), F32),
                        pltpu.VMEM((n_past, hr, 128), F32),
                        pltpu.VMEM((n_past, hr, 128), F32),
                        pltpu.VMEM((n_past, hr, ATT_DIM), F32),
                        pltpu.VMEM((n_past, 8, ATT_W), F32),
                        pltpu.VMEM((128, ATT_W), F32),
                        pltpu.VMEM((128, ATT_W), BF16),
                        pltpu.VMEM((128, ATT_W), BF16)])
    return pl.pallas_call(
        _moba_sample_body,
        grid_spec=grid_spec,
        out_shape=jax.ShapeDtypeStruct((s, rows, ATT_W), BF16),
        compiler_params=_params("arbitrary", "arbitrary"),
        name="moba_sample",
    )(page_table, z, z, z, ck, ck, cv, cv)


def _retention_body(*refs, rows, has_state):
    if has_state:
        (q_ref, k_ref, v_ref, g_ref, cos_ref, sin_ref, intra_ref, qd_ref, kd_ref, cd_ref, gn_ref,
         r0_ref, o_ref, rout_ref, r_scr) = refs
    else:
        (q_ref, k_ref, v_ref, g_ref, cos_ref, sin_ref, intra_ref, qd_ref, kd_ref, cd_ref, gn_ref,
         o_ref, rout_ref, r_scr) = refs
    ci = pl.program_id(2)
    c = intra_ref.shape[0]

    @pl.when(ci == 0)
    def _():
        if has_state:
            r_scr[...] = r0_ref[...]
        else:
            r_scr[...] = jnp.zeros_like(r_scr)

    def pad(x):
        if rows == c:
            return x
        return jnp.concatenate([x, jnp.zeros((c - rows, x.shape[1]), x.dtype)], axis=0)

    cos = pad(cos_ref[...])
    sin = pad(sin_ref[...])

    def rotate(x):
        lane = lax.broadcasted_iota(jnp.int32, x.shape, 1)
        nxt = pltpu.roll(x, x.shape[1] - 1, axis=1)
        prv = pltpu.roll(x, 1, axis=1)
        return x * cos + jnp.where(lane % 2 == 0, nxt, prv) * sin

    q = rotate(pad(q_ref[...]))
    k = rotate(pad(k_ref[...])) * (RET_DK ** -0.5)
    v = pad(v_ref[...])
    r = r_scr[...]
    qb, kb, vb = q.astype(BF16), k.astype(BF16), v.astype(BF16)
    s = _dot_nt(qb, kb) * intra_ref[...]
    o = _dot(s.astype(BF16), vb) + _dot(qb, r.astype(BF16)) * qd_ref[...]
    r_new = r * cd_ref[...] + _dot_tn((k * kd_ref[...]).astype(BF16), vb)
    r_scr[...] = r_new

    mu = jnp.mean(o, -1, keepdims=True)
    oc = o - mu
    var = jnp.mean(oc * oc, -1, keepdims=True)
    on = oc * lax.rsqrt(var + LN_EPS) * gn_ref[...]
    y = _silu(pad(g_ref[...])) * on
    o_ref[...] = y[0:rows, :].astype(o_ref.dtype)

    @pl.when(ci == pl.num_programs(2) - 1)
    def _():
        rout_ref[...] = r_new


def _retention_tables(t, pos0, c_true, c_pad):
    inv = 1.0 / (10000.0 ** jnp.linspace(0.0, 1.0, RET_DK // 2, dtype=F32))
    pos = pos0 + jnp.arange(t, dtype=jnp.int32)
    ang = pos.astype(F32)[:, None] * inv[None, :]
    cos = jnp.repeat(jnp.cos(ang), 2, axis=-1)
    sin = jnp.stack([-jnp.sin(ang), jnp.sin(ang)], axis=-1).reshape(t, RET_DK)
    log_g = jnp.log1p(-jnp.exp2(-5.0 - jnp.arange(RET_HEADS, dtype=F32)))
    i = jnp.arange(c_pad, dtype=F32)
    live = i < c_true
    diff = i[:, None] - i[None, :]
    intra = jnp.where(diff >= 0, jnp.exp(log_g[:, None, None] * jnp.maximum(diff, 0.0)), 0.0)
    intra = jnp.where(live[None, :, None] & live[None, None, :], intra, 0.0)
    q_decay = jnp.exp(log_g[:, None] * (i[None, :] + 1.0))
    k_decay = jnp.where(live[None, :], jnp.exp(log_g[:, None] * (c_true - 1.0 - i[None, :])), 0.0)
    chunk_decay = jnp.exp(log_g * c_true)
    return (cos, sin, intra, q_decay[:, :, None], k_decay[:, :, None], chunk_decay[:, None, None])


def retention(z, gn_g, state, pos0):
    s, t, _ = z.shape
    c_true = math.gcd(RET_CHUNK, t)
    c_pad = RET_CHUNK
    n_chunks = t // c_true
    cos, sin, intra, qd, kd, cd = _retention_tables(t, pos0, c_true, c_pad)
    has_state = state is not None

    def zspec(off, width):
        return pl.BlockSpec((None, c_true, width), lambda i, h, ci: (i, ci, off // width + h))

    in_specs = [zspec(OFF_BQ, RET_DK), zspec(OFF_BK, RET_DK), zspec(OFF_BV, RET_DV), zspec(OFF_BG, RET_DV),
                pl.BlockSpec((c_true, RET_DK), lambda i, h, ci: (ci, 0)),
                pl.BlockSpec((c_true, RET_DK), lambda i, h, ci: (ci, 0)),
                pl.BlockSpec((None, c_pad, c_pad), lambda i, h, ci: (h, 0, 0)),
                pl.BlockSpec((None, c_pad, 1), lambda i, h, ci: (h, 0, 0)),
                pl.BlockSpec((None, c_pad, 1), lambda i, h, ci: (h, 0, 0)),
                pl.BlockSpec((None, 1, 1), lambda i, h, ci: (h, 0, 0)),
                pl.BlockSpec((1, RET_DV), lambda i, h, ci: (0, h))]
    args = [z, z, z, z, cos, sin, intra, qd, kd, cd, gn_g.reshape(1, -1)]
    if has_state:
        in_specs.append(pl.BlockSpec((None, None, RET_DK, RET_DV), lambda i, h, ci: (i, h, 0, 0)))
        args.append(state)
    return pl.pallas_call(
        functools.partial(_retention_body, rows=c_true, has_state=has_state),
        grid=(s, RET_HEADS, n_chunks),
        in_specs=in_specs,
        out_specs=[pl.BlockSpec((None, c_true, RET_DV), lambda i, h, ci: (i, ci, h)),
                   pl.BlockSpec((None, None, RET_DK, RET_DV), lambda i, h, ci: (i, h, 0, 0))],
        out_shape=[jax.ShapeDtypeStruct((s, t, RET_HEADS * RET_DV), BF16),
                   jax.ShapeDtypeStruct((s, RET_HEADS, RET_DK, RET_DV), F32)],
        scratch_shapes=[pltpu.VMEM((RET_DK, RET_DV), F32)],
        compiler_params=_params("arbitrary", "arbitrary", "arbitrary"),
        name="retention",
    )(*args)


def _conformer_body(*refs, has_state):
    if has_state:
        a_ref, g_ref, w_ref, b_ref, lg_ref, lb_ref, st_ref, o_ref, cn_ref, buf = refs
    else:
        a_ref, g_ref, w_ref, b_ref, lg_ref, lb_ref, o_ref, cn_ref, buf = refs
    ti = pl.program_id(1)
    nb, t, ch = a_ref.shape
    keep = CONV_WIDTH - 1
    lead = HIST_ROWS - keep

    @pl.when(ti == 0)
    def _():
        buf[:, 0:HIST_ROWS, :] = jnp.zeros((nb, HIST_ROWS, ch), F32)
        if has_state:
            buf[:, lead:HIST_ROWS, :] = st_ref[...]

    buf[:, HIST_ROWS:HIST_ROWS + t, :] = a_ref[...] * jax.nn.sigmoid(g_ref[...])
    acc = jnp.zeros((nb, t, ch), F32) + b_ref[...]
    for j in range(CONV_WIDTH):
        acc = acc + w_ref[j:j + 1, :] * buf[:, lead + j:lead + j + t, :]
    y = _silu(_layer_norm(acc, lg_ref[...], lb_ref[...]))
    o_ref[...] = y.astype(o_ref.dtype)

    hist = buf[:, t:t + HIST_ROWS, :]
    buf[:, 0:HIST_ROWS, :] = hist

    @pl.when(ti == pl.num_programs(1) - 1)
    def _():
        cn_ref[...] = hist[:, lead:HIST_ROWS, :]


def conformer_conv(z, conv_w, conv_b, ln_g, ln_b, state, nb, t):
    s, tt, _ = z.shape
    keep = CONV_WIDTH - 1
    has_state = state is not None
    acol = OFF_C // CONV_CH
    in_specs = [pl.BlockSpec((nb, t, CONV_CH), lambda i, r: (i, r, acol)),
                pl.BlockSpec((nb, t, CONV_CH), lambda i, r: (i, r, acol + 1)),
                pl.BlockSpec((CONV_WIDTH, CONV_CH), lambda i, r: (0, 0)),
                pl.BlockSpec((1, CONV_CH), lambda i, r: (0, 0)),
                pl.BlockSpec((1, CONV_CH), lambda i, r: (0, 0)),
                pl.BlockSpec((1, CONV_CH), lambda i, r: (0, 0))]
    args = [z, z, conv_w, conv_b.reshape(1, -1), ln_g.reshape(1, -1), ln_b.reshape(1, -1)]
    if has_state:
        in_specs.append(pl.BlockSpec((nb, keep, CONV_CH), lambda i, r: (i, 0, 0)))
        args.append(state)
    return pl.pallas_call(
        functools.partial(_conformer_body, has_state=has_state),
        grid=(s // nb, tt // t),
        in_specs=in_specs,
        out_specs=[pl.BlockSpec((nb, t, CONV_CH), lambda i, r: (i, r, 0)),
                   pl.BlockSpec((nb, keep, CONV_CH), lambda i, r: (i, 0, 0))],
        out_shape=[jax.ShapeDtypeStruct((s, tt, CONV_CH), BF16),
                   jax.ShapeDtypeStruct((s, keep, CONV_CH), F32)],
        scratch_shapes=[pltpu.VMEM((nb, HIST_ROWS + t, CONV_CH), F32)],
        compiler_params=_params("arbitrary", "arbitrary"),
        name="conformer_conv",
    )(*args)


def _merge_body(ya_ref, yb_ref, yc_ref, w_ref, ga_ref, gb_ref, gc_ref, o_ref):
    nb, t, tn = o_ref.shape

    def branch(y_ref, i, g_ref):
        y = y_ref[...].reshape(nb * t, y_ref.shape[2])
        return jax.nn.sigmoid(g_ref[...]) * _dot(y, w_ref[i]).reshape(nb, t, tn)

    o_ref[...] = (branch(ya_ref, 0, ga_ref) + branch(yb_ref, 1, gb_ref)
                  + branch(yc_ref, 2, gc_ref)).astype(o_ref.dtype)


def gated_merge(ya, yb, yc, z, w_branch, nb, t, tn):
    s, tt, bw = ya.shape
    d = w_branch.shape[2]
    gcol = OFF_GA // tn
    per = d // tn
    yspec = pl.BlockSpec((nb, t, bw), lambda i, r, j: (i, r, 0))

    def gspec(which):
        return pl.BlockSpec((nb, t, tn), lambda i, r, j: (i, r, gcol + which * per + j))

    return pl.pallas_call(
        _merge_body,
        grid=(s // nb, tt // t, d // tn),
        in_specs=[yspec, yspec, yspec,
                  pl.BlockSpec((3, bw, tn), lambda i, r, j: (0, 0, j)),
                  gspec(0), gspec(1), gspec(2)],
        out_specs=pl.BlockSpec((nb, t, tn), lambda i, r, j: (i, r, j)),
        out_shape=jax.ShapeDtypeStruct((s, tt, d), BF16),
        compiler_params=_params("arbitrary", "arbitrary", "arbitrary"),
        name="gated_merge",
    )(ya, yb, yc, w_branch, z, z, z)


def _outproj_body(x_ref, m_ref, gate_ref, w_ref, lg_ref, lb_ref, o_ref, *, alpha):
    nb, t, d = x_ref.shape
    y = _dot(m_ref[...].reshape(nb * t, m_ref.shape[2]), w_ref[...]).reshape(nb, t, d)
    o_ref[...] = _layer_norm(alpha * x_ref[...] + (1.0 + gate_ref[...]) * y, lg_ref[...], lb_ref[...])


def outproj_norm(x, merged, ada, gate_chunk, w_o, ln_g, ln_b, alpha, nb, t):
    s, tt, d = x.shape
    tok = lambda i, r: (i, r, 0)
    return pl.pallas_call(
        functools.partial(_outproj_body, alpha=alpha),
        grid=(s // nb, tt // t),
        in_specs=[pl.BlockSpec((nb, t, d), tok),
                  pl.BlockSpec((nb, t, merged.shape[2]), tok),
                  pl.BlockSpec((nb, 1, d), lambda i, r: (i, 0, gate_chunk)),
                  pl.BlockSpec(w_o.shape, lambda i, r: (0, 0)),
                  pl.BlockSpec((1, d), lambda i, r: (0, 0)),
                  pl.BlockSpec((1, d), lambda i, r: (0, 0))],
        out_specs=pl.BlockSpec((nb, t, d), tok),
        out_shape=jax.ShapeDtypeStruct((s, tt, d), F32),
        compiler_params=_params("arbitrary", "arbitrary"),
        name="outproj_norm",
    )(x, merged, ada, w_o, ln_g.reshape(1, -1), ln_b.reshape(1, -1))


def _ffn_down_body(g_ref, u_ref, prev_ref, cw_ref, cb_ref, wd_ref, x_ref, gate_ref, lg_ref, lb_ref,
                   o_ref, gbuf, acc, *, alpha, zero_first):
    kk = pl.program_id(2)
    nb, t, tk = g_ref.shape
    keep = FFN_CONV_WIDTH - 1
    lead = FFN_HIST_ROWS - keep
    prev_rows = prev_ref.shape[1]

    prev = prev_ref[:, prev_rows - keep:prev_rows, :]
    if zero_first:
        prev = jnp.where(pl.program_id(1) == 0, 0.0, prev)
    gbuf[:, lead:FFN_HIST_ROWS, :] = prev
    gbuf[:, FFN_HIST_ROWS:FFN_HIST_ROWS + t, :] = g_ref[...]
    conv = jnp.zeros((nb, t, tk), F32) + cb_ref[...]
    for j in range(FFN_CONV_WIDTH):
        conv = conv + cw_ref[j:j + 1, :] * gbuf[:, lead + j:lead + j + t, :]
    act = (_silu(conv) * u_ref[...]).reshape(nb * t, tk).astype(BF16)
    part = _dot(act, wd_ref[...])

    @pl.when(kk == 0)
    def _():
        acc[...] = part

    @pl.when(kk > 0)
    def _():
        acc[...] = acc[...] + part

    @pl.when(kk == pl.num_programs(2) - 1)
    def _():
        y = acc[...].reshape(o_ref.shape)
        o_ref[...] = _layer_norm(alpha * x_ref[...] + (1.0 + gate_ref[...]) * y, lg_ref[...], lb_ref[...])


def ffn_down_norm(gu, prev, x, ada, gate_chunk, conv_w, conv_b, w_down, ln_g, ln_b, alpha, nb, t, tk,
                  prev_is_state):
    s, tt, f2 = gu.shape
    f = f2 // 2
    d = x.shape[2]
    nk = f // tk
    if prev_is_state:
        assert tt == t
        prev_spec = pl.BlockSpec((nb, prev.shape[1], tk), lambda i, r, k: (i, 0, k))
    else:
        assert nb == 1
        per = t // FFN_HIST_ROWS
        prev_spec = pl.BlockSpec((nb, FFN_HIST_ROWS, tk),
                                 lambda i, r, k: (i, jnp.maximum(r * per - 1, 0), k))
    tok = lambda i, r, k: (i, r, 0)
    return pl.pallas_call(
        functools.partial(_ffn_down_body, alpha=alpha, zero_first=not prev_is_state),
        grid=(s // nb, tt // t, nk),
        in_specs=[pl.BlockSpec((nb, t, tk), lambda i, r, k: (i, r, k)),
                  pl.BlockSpec((nb, t, tk), lambda i, r, k: (i, r, nk + k)),
                  prev_spec,
                  pl.BlockSpec((FFN_CONV_WIDTH, tk), lambda i, r, k: (0, k)),
                  pl.BlockSpec((1, tk), lambda i, r, k: (0, k)),
                  pl.BlockSpec((tk, d), lambda i, r, k: (k, 0)),
                  pl.BlockSpec((nb, t, d), tok),
                  pl.BlockSpec((nb, 1, d), lambda i, r, k: (i, 0, gate_chunk)),
                  pl.BlockSpec((1, d), lambda i, r, k: (0, 0)),
                  pl.BlockSpec((1, d), lambda i, r, k: (0, 0))],
        out_specs=pl.BlockSpec((nb, t, d), tok),
        out_shape=jax.ShapeDtypeStruct((s, tt, d), F32),
        scratch_shapes=[pltpu.VMEM((nb, FFN_HIST_ROWS + t, tk), F32),
                        pltpu.VMEM((nb * t, d), F32)],
        compiler_params=_params("arbitrary", "arbitrary", "arbitrary"),
        name="ffn_down_norm",
    )(gu, gu, prev, conv_w, conv_b.reshape(1, -1), w_down, x, ada, ln_g.reshape(1, -1), ln_b.reshape(1, -1))


def _tile(s, t, rows):
    if t >= rows:
        assert t % rows == 0
        return 1, rows
    nb = max(1, min(s, rows // t))
    assert s % nb == 0
    return nb, t


def trunk_layer(x, ada, attend, conv_state, ret_state, ffn_state, pos0, alpha, p):
    s, t, d = x.shape
    nb, tr = _tile(s, t, 1024 if t >= 1024 else 512)
    z = modulated_matmul(x, ada, 1, 0, p["w_in"], nb, tr, 512)
    ya = attend(z)
    yb, ret_new = retention(z, p["ret_gn_g"], ret_state, pos0)
    nbc, trc = _tile(s, t, 256) if t >= 256 else _tile(s, t, 128)
    yc, conv_new = conformer_conv(z, p["conv_w"], p["conv_b"], p["conv_ln_g"], p["conv_ln_b"],
                                  conv_state, nbc, trc)
    nbm, trm = _tile(s, t, 512)
    merged = gated_merge(ya, yb, yc, z, p["w_branch"], nbm, trm, 512)
    x1 = outproj_norm(x, merged, ada, 2, p["w_o"], p["ln1_g"], p["ln1_b"], alpha, nbm, trm)
    gu = modulated_matmul(x1, ada, 4, 3, p["w_up"], nb, tr, 512)
    f = gu.shape[2] // 2
    prev = gu if ffn_state is None else ffn_state
    x2 = ffn_down_norm(gu, prev, x1, ada, 5, p["ffn_conv_w"], p["ffn_conv_b"], p["w_down"],
                       p["ln2_g"], p["ln2_b"], alpha, nbm, trm, 512, ffn_state is not None)
    k_new = z[:, :, OFF_AK:OFF_AK + ATT_W].reshape(s, t, ATT_HEADS, ATT_DIM)
    v_new = z[:, :, OFF_AV:OFF_AV + ATT_W].reshape(s, t, ATT_HEADS, ATT_DIM)
    ffn_new = gu[:, t - (FFN_CONV_WIDTH - 1):, :f]
    return x2, k_new, v_new, conv_new, ret_new, ffn_new


def kernel(x_prompt, x_sample, cache_k, cache_v, state_conv, state_ret, state_ffn, page_table,
           c_prompt, c_sample, w_ada, b_ada, w_in, w_branch, w_o, ln1_g, ln1_b, ret_gn_g,
           conv_w, conv_b, conv_ln_g, conv_ln_b, w_up, ffn_conv_w, ffn_conv_b, w_down, ln2_g, ln2_b):
    depth = w_ada.shape[0]
    b = x_prompt.shape[0]
    sb = x_sample.shape[0]
    alpha = (2 * depth) ** 0.25
    past = page_table.shape[1] * PAGE_SIZE

    c_all = jnp.concatenate([c_prompt, c_sample], axis=0)
    pad = (-c_all.shape[0]) % 8
    ada = ada_all(jnp.pad(c_all, ((0, pad), (0, 0))), w_ada, b_ada)
    ada_p = ada[:, :b, None, :]
    ada_s = ada[:, b:b + sb, None, :]

    xp, xs = x_prompt, x_sample
    outs_p, outs_s = [], []
    for l in range(depth):
        p = {"w_in": w_in[l].astype(BF16), "w_branch": w_branch[l].astype(BF16), "w_o": w_o[l].astype(BF16),
             "w_up": w_up[l].astype(BF16), "w_down": w_down[l].astype(BF16),
             "ln1_g": ln1_g[l], "ln1_b": ln1_b[l], "ret_gn_g": ret_gn_g[l], "conv_w": conv_w[l],
             "conv_b": conv_b[l], "conv_ln_g": conv_ln_g[l], "conv_ln_b": conv_ln_b[l],
             "ffn_conv_w": ffn_conv_w[l], "ffn_conv_b": ffn_conv_b[l], "ln2_g": ln2_g[l], "ln2_b": ln2_b[l]}
        xp, *rest_p = trunk_layer(xp, ada_p[l], moba_prompt, None, None, None, 0, alpha, p)
        attend_s = functools.partial(moba_sample, cache_k=cache_k, cache_v=cache_v,
                                     page_table=page_table, layer=l)
        xs, *rest_s = trunk_layer(xs, ada_s[l], attend_s, state_conv[l], state_ret[l], state_ffn[l],
                                  past, alpha, p)
        outs_p.append(rest_p)
        outs_s.append(rest_s)
    stack = lambda outs, i: jnp.stack([o[i] for o in outs])
    return (xp, xs, *(stack(outs_p, i) for i in range(5)), *(stack(outs_s, i) for i in range(5)))
```

```python
import functools
import math

import jax
import jax.numpy as jnp
from jax import lax
from jax.experimental import pallas as pl
from jax.experimental.pallas import tpu as pltpu

F32 = jnp.float32
BF16 = jnp.bfloat16

PAGE_SIZE = 128
ATT_HEADS = 8
ATT_DIM = 128
ATT_W = ATT_HEADS * ATT_DIM
MOBA_BLOCK = 256
MOBA_TOPK = 3
MOBA_CHUNK_BLOCKS = 4
RET_HEADS = 4
RET_DK = 256
RET_DV = 256
RET_CHUNK = 128
CONV_CH = 1024
CONV_WIDTH = 31
FFN_CONV_WIDTH = 3
LN_EPS = 1e-5
NEG_BIG = -1e30

OFF_AQ, OFF_AK, OFF_AV = 0, ATT_W, 2 * ATT_W
OFF_BQ = 3 * ATT_W
OFF_BK = OFF_BQ + RET_HEADS * RET_DK
OFF_BV = OFF_BK + RET_HEADS * RET_DK
OFF_BG = OFF_BV + RET_HEADS * RET_DV
OFF_C = OFF_BG + RET_HEADS * RET_DV
OFF_GA = OFF_C + 2 * CONV_CH

VMEM_LIMIT_BYTES = 56 * 2**20
HIST_ROWS = 32
FFN_HIST_ROWS = 8


def _params(*sem):
    return pltpu.CompilerParams(dimension_semantics=sem, vmem_limit_bytes=VMEM_LIMIT_BYTES)


def _dot(a, b):
    return jnp.dot(a, b, preferred_element_type=F32)


def _dot_nt(a, b, precision=None):
    return lax.dot_general(a, b, (((1,), (1,)), ((), ())), precision=precision,
                           preferred_element_type=F32)


def _dot_tn(a, b):
    return lax.dot_general(a, b, (((0,), (0,)), ((), ())), preferred_element_type=F32)


def _silu(x):
    return x * jax.nn.sigmoid(x)


def _layer_norm(x, g, b):
    mu = jnp.mean(x, -1, keepdims=True)
    xc = x - mu
    var = jnp.mean(xc * xc, -1, keepdims=True)
    return xc * lax.rsqrt(var + LN_EPS) * g + b


def _ada_body(c_ref, w_ref, b_ref, o_ref):
    a = _silu(c_ref[...]).astype(BF16)
    o_ref[...] = _dot(a, w_ref[...].astype(BF16)) + b_ref[...]


def ada_all(c_all, w_ada, b_ada, tn=1024):
    depth, d, n = w_ada.shape
    r = c_all.shape[0]
    return pl.pallas_call(
        _ada_body,
        grid=(depth, n // tn),
        in_specs=[pl.BlockSpec((r, d), lambda l, j: (0, 0)),
                  pl.BlockSpec((None, d, tn), lambda l, j: (l, 0, j)),
                  pl.BlockSpec((None, 1, tn), lambda l, j: (l, 0, j))],
        out_specs=pl.BlockSpec((None, r, tn), lambda l, j: (l, 0, j)),
        out_shape=jax.ShapeDtypeStruct((depth, r, n), F32),
        compiler_params=_params("arbitrary", "arbitrary"),
        name="ada",
    )(c_all, w_ada, b_ada.reshape(depth, 1, n))


def _modmm_body(x_ref, sc_ref, sh_ref, w_ref, o_ref, h_ref):
    @pl.when(pl.program_id(2) == 0)
    def _():
        h = x_ref[...] * (1.0 + sc_ref[...]) + sh_ref[...]
        h_ref[...] = h.reshape(h_ref.shape).astype(BF16)

    o_ref[...] = _dot(h_ref[...], w_ref[...]).reshape(o_ref.shape)


def modulated_matmul(x, ada, sc_chunk, sh_chunk, w, nb, t, tn):
    s, tt, d = x.shape
    n = w.shape[1]
    return pl.pallas_call(
        _modmm_body,
        grid=(s // nb, tt // t, n // tn),
        in_specs=[pl.BlockSpec((nb, t, d), lambda i, r, j: (i, r, 0)),
                  pl.BlockSpec((nb, 1, d), lambda i, r, j: (i, 0, sc_chunk)),
                  pl.BlockSpec((nb, 1, d), lambda i, r, j: (i, 0, sh_chunk)),
                  pl.BlockSpec((d, tn), lambda i, r, j: (0, j))],
        out_specs=pl.BlockSpec((nb, t, tn), lambda i, r, j: (i, r, j)),
        out_shape=jax.ShapeDtypeStruct((s, tt, n), F32),
        scratch_shapes=[pltpu.VMEM((nb * t, d), BF16)],
        compiler_params=_params("arbitrary", "arbitrary", "arbitrary"),
        name="modmm",
    )(x, ada, ada, w)


def _select_blocks(score, n_past, n_blocks):
    lane = lax.broadcasted_iota(jnp.int32, score.shape, 1)
    past = lane < n_past
    s = jnp.where(past, score, -jnp.inf)
    rank = jnp.zeros(score.shape, jnp.int32)
    for m in range(n_blocks):
        sm = s[:, m:m + 1]
        beats = jnp.where(sm > s, 1, jnp.where(sm == s, jnp.where(lane > m, 1, 0), 0))
        rank = rank + beats
    return jnp.where(past, jnp.where(rank < MOBA_TOPK, 1.0, 0.0), 0.0)


def _moba_prompt_body(q_ref, k_ref, v_ref, o_ref, kb_ref, vb_ref, km_ref, m_scr, l_scr, acc_scr):
    qi = pl.program_id(2)
    n_blocks = k_ref.shape[0] // MOBA_BLOCK
    scale = ATT_DIM ** -0.5

    @pl.when(qi == 0)
    def _():
        kb_ref[...] = k_ref[...].astype(BF16)
        vb_ref[...] = v_ref[...].astype(BF16)
        km_ref[...] = jnp.zeros_like(km_ref)
        for n in range(n_blocks):
            km_ref[n:n + 1, :] = jnp.mean(k_ref[n * MOBA_BLOCK:(n + 1) * MOBA_BLOCK, :],
                                          axis=0, keepdims=True)

    q = q_ref[...]
    qb = q.astype(BF16)
    score = _dot_nt(q, km_ref[...], precision=lax.Precision.HIGHEST)
    bias = (_select_blocks(score, qi, n_blocks) - 1.0) * (-NEG_BIG)

    start = pl.multiple_of(qi * MOBA_BLOCK, MOBA_BLOCK)
    row = lax.broadcasted_iota(jnp.int32, (MOBA_BLOCK, MOBA_BLOCK), 0)
    col = lax.broadcasted_iota(jnp.int32, (MOBA_BLOCK, MOBA_BLOCK), 1)
    s0 = _dot_nt(qb, kb_ref[pl.ds(start, MOBA_BLOCK), :]) * scale
    s0 = jnp.where(col <= row, s0, NEG_BIG)
    m0 = jnp.max(s0, axis=1, keepdims=True)
    p0 = jnp.exp(s0 - m0)
    m_scr[...] = m0
    l_scr[...] = jnp.sum(p0, axis=1, keepdims=True)
    acc_scr[...] = _dot(p0.astype(BF16), vb_ref[pl.ds(start, MOBA_BLOCK), :])

    span = MOBA_CHUNK_BLOCKS * MOBA_BLOCK
    for c in range(n_blocks // MOBA_CHUNK_BLOCKS):
        @pl.when(c * MOBA_CHUNK_BLOCKS < qi)
        def _():
            s = _dot_nt(qb, kb_ref[c * span:(c + 1) * span, :]) * scale
            s = jnp.concatenate(
                [s[:, j * MOBA_BLOCK:(j + 1) * MOBA_BLOCK]
                 + bias[:, c * MOBA_CHUNK_BLOCKS + j:c * MOBA_CHUNK_BLOCKS + j + 1]
                 for j in range(MOBA_CHUNK_BLOCKS)], axis=1)
            m = m_scr[...]
            m_new = jnp.maximum(m, jnp.max(s, axis=1, keepdims=True))
            alpha = jnp.exp(m - m_new)
            p = jnp.exp(s - m_new)
            l_scr[...] = alpha * l_scr[...] + jnp.sum(p, axis=1, keepdims=True)
            acc_scr[...] = alpha * acc_scr[...] + _dot(p.astype(BF16), vb_ref[c * span:(c + 1) * span, :])
            m_scr[...] = m_new

    o_ref[...] = (acc_scr[...] / l_scr[...]).astype(o_ref.dtype)


def moba_prompt(z):
    b, t, _ = z.shape
    assert t % (MOBA_BLOCK * MOBA_CHUNK_BLOCKS) == 0
    nq = t // MOBA_BLOCK
    kcol = OFF_AK // ATT_DIM
    vcol = OFF_AV // ATT_DIM
    return pl.pallas_call(
        _moba_prompt_body,
        grid=(b, ATT_HEADS, nq),
        in_specs=[pl.BlockSpec((None, MOBA_BLOCK, ATT_DIM), lambda i, h, q: (i, q, h)),
                  pl.BlockSpec((None, t, ATT_DIM), lambda i, h, q: (i, 0, kcol + h)),
                  pl.BlockSpec((None, t, ATT_DIM), lambda i, h, q: (i, 0, vcol + h))],
        out_specs=pl.BlockSpec((None, MOBA_BLOCK, ATT_DIM), lambda i, h, q: (i, q, h)),
        out_shape=jax.ShapeDtypeStruct((b, t, ATT_W), BF16),
        scratch_shapes=[pltpu.VMEM((t, ATT_DIM), BF16), pltpu.VMEM((t, ATT_DIM), BF16),
                        pltpu.VMEM((128, ATT_DIM), F32),
                        pltpu.VMEM((MOBA_BLOCK, 1), F32), pltpu.VMEM((MOBA_BLOCK, 1), F32),
                        pltpu.VMEM((MOBA_BLOCK, ATT_DIM), F32)],
        compiler_params=_params("arbitrary", "arbitrary", "arbitrary"),
        name="moba_prompt",
    )(z, z, z)


def _attend_partial(qbd_b, kb, vb, mask):
    s = _dot_nt(qbd_b, kb) * (ATT_DIM ** -0.5)
    if mask is not None:
        s = jnp.where(mask, s, NEG_BIG)
    m = jnp.max(s, axis=1, keepdims=True)
    p = jnp.exp(s - m)
    l = jnp.sum(p, axis=1, keepdims=True)
    full = _dot(p.astype(BF16), vb)
    rows = full.shape[0] // ATT_HEADS
    acc = jnp.concatenate(
        [full[h * rows:(h + 1) * rows, h * ATT_DIM:(h + 1) * ATT_DIM] for h in range(ATT_HEADS)],
        axis=0)
    return m, l, acc


def _page_rows(page_ref):
    return jnp.concatenate(
        [page_ref[pl.ds(h, PAGE_SIZE, stride=ATT_HEADS), :] for h in range(ATT_HEADS)], axis=1)


def _moba_sample_body(pt_ref, q_ref, kn_ref, vn_ref, *refs, blocks_per_step):
    del pt_ref
    pages_per_block = MOBA_BLOCK // PAGE_SIZE
    n_pg = blocks_per_step * pages_per_block
    k_refs, v_refs = refs[:n_pg], refs[n_pg:2 * n_pg]
    o_ref, qbd_ref, m_ref, l_ref, acc_ref, ks_ref, km_ref, ko_ref, vo_ref = refs[2 * n_pg:]
    step = pl.program_id(1)
    rows = q_ref.shape[0]
    hr = ATT_HEADS * rows

    @pl.when(step == 0)
    def _():
        q = q_ref[...]
        head = lax.broadcasted_iota(jnp.int32, q.shape, 1) // ATT_DIM
        qbd_ref[...] = jnp.concatenate(
            [jnp.where(head == h, q, 0.0) for h in range(ATT_HEADS)], axis=0)
        ko_ref[...] = jnp.zeros_like(ko_ref)
        vo_ref[...] = jnp.zeros_like(vo_ref)
        ko_ref[0:rows, :] = kn_ref[...].astype(BF16)
        vo_ref[0:rows, :] = vn_ref[...].astype(BF16)

    qbd_b = qbd_ref[...].astype(BF16)
    for jb in range(blocks_per_step):
        blk = step * blocks_per_step + jb
        pages = range(jb * pages_per_block, (jb + 1) * pages_per_block)
        kblk = jnp.concatenate([_page_rows(k_refs[pg]) for pg in pages], axis=0)
        vblk = jnp.concatenate([_page_rows(v_refs[pg]) for pg in pages], axis=0)
        m, l, acc = _attend_partial(qbd_b, kblk.astype(BF16), vblk.astype(BF16), None)
        m_ref[blk] = jnp.broadcast_to(m, (hr, 128))
        l_ref[blk] = jnp.broadcast_to(l, (hr, 128))
        acc_ref[blk] = acc
        ks_ref[blk] = jnp.sum(kblk.reshape(MOBA_BLOCK // 8, 8, ATT_W), axis=0)

    @pl.when(step == pl.num_programs(1) - 1)
    def _():
        nb_static = m_ref.shape[0]
        km_ref[...] = jnp.zeros_like(km_ref)
        for j in range(nb_static):
            km_ref[j:j + 1, :] = jnp.sum(ks_ref[j], axis=0, keepdims=True) * (1.0 / MOBA_BLOCK)
        score = _dot_nt(qbd_ref[...], km_ref[...], precision=lax.Precision.HIGHEST)
        sel = _select_blocks(score, nb_static, nb_static)
        qrow = lax.broadcasted_iota(jnp.int32, (hr, 128), 0) % rows
        kcol = lax.broadcasted_iota(jnp.int32, (hr, 128), 1)
        m_o, l_o, acc_o = _attend_partial(qbd_b, ko_ref[...], vo_ref[...], kcol <= qrow)
        m_tot = m_o
        for j in range(nb_static):
            m_tot = jnp.maximum(m_tot, jnp.where(sel[:, j:j + 1] > 0.0, m_ref[j][:, 0:1], NEG_BIG))
        w_o = jnp.exp(m_o - m_tot)
        num = w_o * acc_o
        den = w_o * l_o
        for j in range(nb_static):
            w = jnp.where(sel[:, j:j + 1] > 0.0, jnp.exp(m_ref[j][:, 0:1] - m_tot), 0.0)
            num = num + w * acc_ref[j]
            den = den + w * l_ref[j][:, 0:1]
        out = num / den
        for h in range(ATT_HEADS):
            o_ref[:, h * ATT_DIM:(h + 1) * ATT_DIM] = out[h * rows:(h + 1) * rows, :].astype(o_ref.dtype)


def moba_sample(z, cache_k, cache_v, page_table, layer, blocks_per_step=4):
    s, rows, _ = z.shape
    n_pages = page_table.shape[1]
    pages_per_block = MOBA_BLOCK // PAGE_SIZE
    assert MOBA_BLOCK % PAGE_SIZE == 0 and n_pages % pages_per_block == 0 and rows <= 128
    n_past = n_pages // pages_per_block
    assert n_past <= 128 and n_past % blocks_per_step == 0
    n_pg = blocks_per_step * pages_per_block
    hr = ATT_HEADS * rows

    depth, n_phys = cache_k.shape[0], cache_k.shape[1]
    ck = cache_k.reshape(depth, n_phys, PAGE_SIZE * ATT_HEADS, ATT_DIM)
    cv = cache_v.reshape(depth, n_phys, PAGE_SIZE * ATT_HEADS, ATT_DIM)

    def page_spec(which):
        return pl.BlockSpec((None, None, PAGE_SIZE * ATT_HEADS, ATT_DIM),
                            lambda i, n, pt: (layer, pt[i, n_pg * n + which], 0, 0))

    def z_spec(col):
        return pl.BlockSpec((None, rows, ATT_W), lambda i, n, pt: (i, 0, col))

    page_specs = [page_spec(pg) for pg in range(n_pg)]
    grid_spec = pltpu.PrefetchScalarGridSpec(
        num_scalar_prefetch=1,
        grid=(s, n_past // blocks_per_step),
        in_specs=[z_spec(OFF_AQ // ATT_W), z_spec(OFF_AK // ATT_W), z_spec(OFF_AV // ATT_W)]
        + page_specs + page_specs,
        out_specs=pl.BlockSpec((None, rows, ATT_W), lambda i, n, pt: (i, 0, 0)),
        scratch_shapes=[pltpu.VMEM((hr, ATT_W), F32),
                        pltpu.VMEM((n_past, hr, 128), F32),
                        pltpu.VMEM((n_past, hr, 128), F32),
                        pltpu.VMEM((n_past, hr, ATT_DIM), F32),
                        pltpu.VMEM((n_past, 8, ATT_W), F32),
                        pltpu.VMEM((128, ATT_W), F32),
                        pltpu.VMEM((128, ATT_W), BF16),
                        pltpu.VMEM((128, ATT_W), BF16)])
    return pl.pallas_call(
        functools.partial(_moba_sample_body, blocks_per_step=blocks_per_step),
        grid_spec=grid_spec,
        out_shape=jax.ShapeDtypeStruct((s, rows, ATT_W), BF16),
        compiler_params=_params("arbitrary", "arbitrary"),
        name="moba_sample",
    )(page_table, z, z, z, *([ck] * n_pg), *([cv] * n_pg))


def _retention_body(*refs, rows, has_state):
    if has_state:
        (q_ref, k_ref, v_ref, g_ref, cos_ref, sin_ref, intra_ref, qd_ref, kd_ref, cd_ref, gn_ref,
         r0_ref, o_ref, rout_ref, r_scr) = refs
    else:
        (q_ref, k_ref, v_ref, g_ref, cos_ref, sin_ref, intra_ref, qd_ref, kd_ref, cd_ref, gn_ref,
         o_ref, rout_ref, r_scr) = refs
    ci = pl.program_id(1)
    c = intra_ref.shape[1]

    @pl.when(ci == 0)
    def _():
        if has_state:
            r_scr[...] = r0_ref[...]
        else:
            r_scr[...] = jnp.zeros_like(r_scr)

    def pad(x):
        if rows == c:
            return x
        return jnp.concatenate([x, jnp.zeros((c - rows, x.shape[1]), x.dtype)], axis=0)

    cos = pad(cos_ref[...])
    sin = pad(sin_ref[...])
    even = lax.broadcasted_iota(jnp.int32, cos.shape, 1) % 2 == 0

    def rotate(x):
        nxt = pltpu.roll(x, x.shape[1] - 1, axis=1)
        prv = pltpu.roll(x, 1, axis=1)
        return x * cos + jnp.where(even, nxt, prv) * sin

    for h in range(RET_HEADS):
        qk = slice(h * RET_DK, (h + 1) * RET_DK)
        vv = slice(h * RET_DV, (h + 1) * RET_DV)
        q = rotate(pad(q_ref[:, qk]))
        k = rotate(pad(k_ref[:, qk])) * (RET_DK ** -0.5)
        v = pad(v_ref[:, vv])
        r = r_scr[h]
        qb, kb, vb = q.astype(BF16), k.astype(BF16), v.astype(BF16)
        s = _dot_nt(qb, kb) * intra_ref[h]
        o = _dot(s.astype(BF16), vb) + _dot(qb, r.astype(BF16)) * qd_ref[h]
        r_new = r * cd_ref[h] + _dot_tn((k * kd_ref[h]).astype(BF16), vb)
        r_scr[h] = r_new

        mu = jnp.mean(o, -1, keepdims=True)
        oc = o - mu
        var = jnp.mean(oc * oc, -1, keepdims=True)
        on = oc * lax.rsqrt(var + LN_EPS) * gn_ref[:, vv]
        y = _silu(pad(g_ref[:, vv])) * on
        o_ref[:, vv] = y[0:rows, :].astype(o_ref.dtype)

    @pl.when(ci == pl.num_programs(1) - 1)
    def _():
        rout_ref[...] = r_scr[...]


def _retention_tables(t, pos0, c_true, c_pad):
    inv = 1.0 / (10000.0 ** jnp.linspace(0.0, 1.0, RET_DK // 2, dtype=F32))
    pos = pos0 + jnp.arange(t, dtype=jnp.int32)
    ang = pos.astype(F32)[:, None] * inv[None, :]
    cos = jnp.repeat(jnp.cos(ang), 2, axis=-1)
    sin = jnp.stack([-jnp.sin(ang), jnp.sin(ang)], axis=-1).reshape(t, RET_DK)
    log_g = jnp.log1p(-jnp.exp2(-5.0 - jnp.arange(RET_HEADS, dtype=F32)))
    i = jnp.arange(c_pad, dtype=F32)
    live = i < c_true
    diff = i[:, None] - i[None, :]
    intra = jnp.where(diff >= 0, jnp.exp(log_g[:, None, None] * jnp.maximum(diff, 0.0)), 0.0)
    intra = jnp.where(live[None, :, None] & live[None, None, :], intra, 0.0)
    q_decay = jnp.exp(log_g[:, None] * (i[None, :] + 1.0))
    k_decay = jnp.where(live[None, :], jnp.exp(log_g[:, None] * (c_true - 1.0 - i[None, :])), 0.0)
    chunk_decay = jnp.exp(log_g * c_true)
    return (cos, sin, intra, q_decay[:, :, None], k_decay[:, :, None], chunk_decay[:, None, None])


def retention(z, gn_g, state, pos0):
    s, t, _ = z.shape
    c_true = math.gcd(RET_CHUNK, t)
    c_pad = RET_CHUNK
    n_chunks = t // c_true
    cos, sin, intra, qd, kd, cd = _retention_tables(t, pos0, c_true, c_pad)
    has_state = state is not None
    qk_w, v_w = RET_HEADS * RET_DK, RET_HEADS * RET_DV

    def zspec(off, width):
        return pl.BlockSpec((None, c_true, width), lambda i, ci: (i, ci, off // width))

    whole = lambda a: pl.BlockSpec(a.shape, lambda i, ci: (0,) * a.ndim)
    state_spec = pl.BlockSpec((None, RET_HEADS, RET_DK, RET_DV), lambda i, ci: (i, 0, 0, 0))
    gn = gn_g.reshape(1, -1)
    in_specs = [zspec(OFF_BQ, qk_w), zspec(OFF_BK, qk_w), zspec(OFF_BV, v_w), zspec(OFF_BG, v_w),
                pl.BlockSpec((c_true, RET_DK), lambda i, ci: (ci, 0)),
                pl.BlockSpec((c_true, RET_DK), lambda i, ci: (ci, 0)),
                whole(intra), whole(qd), whole(kd), whole(cd), whole(gn)]
    args = [z, z, z, z, cos, sin, intra, qd, kd, cd, gn]
    if has_state:
        in_specs.append(state_spec)
        args.append(state)
    return pl.pallas_call(
        functools.partial(_retention_body, rows=c_true, has_state=has_state),
        grid=(s, n_chunks),
        in_specs=in_specs,
        out_specs=[pl.BlockSpec((None, c_true, v_w), lambda i, ci: (i, ci, 0)), state_spec],
        out_shape=[jax.ShapeDtypeStruct((s, t, v_w), BF16),
                   jax.ShapeDtypeStruct((s, RET_HEADS, RET_DK, RET_DV), F32)],
        scratch_shapes=[pltpu.VMEM((RET_HEADS, RET_DK, RET_DV), F32)],
        compiler_params=_params("arbitrary", "arbitrary"),
        name="retention",
    )(*args)


def _conformer_body(*refs, has_state):
    if has_state:
        a_ref, g_ref, w_ref, b_ref, lg_ref, lb_ref, st_ref, o_ref, cn_ref, buf = refs
    else:
        a_ref, g_ref, w_ref, b_ref, lg_ref, lb_ref, o_ref, cn_ref, buf = refs
    ti = pl.program_id(1)
    nb, t, ch = a_ref.shape
    keep = CONV_WIDTH - 1
    lead = HIST_ROWS - keep

    @pl.when(ti == 0)
    def _():
        buf[:, 0:HIST_ROWS, :] = jnp.zeros((nb, HIST_ROWS, ch), F32)
        if has_state:
            buf[:, lead:HIST_ROWS, :] = st_ref[...]

    buf[:, HIST_ROWS:HIST_ROWS + t, :] = a_ref[...] * jax.nn.sigmoid(g_ref[...])
    acc = jnp.zeros((nb, t, ch), F32) + b_ref[...]
    for j in range(CONV_WIDTH):
        acc = acc + w_ref[j:j + 1, :] * buf[:, lead + j:lead + j + t, :]
    y = _silu(_layer_norm(acc, lg_ref[...], lb_ref[...]))
    o_ref[...] = y.astype(o_ref.dtype)

    hist = buf[:, t:t + HIST_ROWS, :]
    buf[:, 0:HIST_ROWS, :] = hist

    @pl.when(ti == pl.num_programs(1) - 1)
    def _():
        cn_ref[...] = hist[:, lead:HIST_ROWS, :]


def conformer_conv(z, conv_w, conv_b, ln_g, ln_b, state, nb, t):
    s, tt, _ = z.shape
    keep = CONV_WIDTH - 1
    has_state = state is not None
    acol = OFF_C // CONV_CH
    in_specs = [pl.BlockSpec((nb, t, CONV_CH), lambda i, r: (i, r, acol)),
                pl.BlockSpec((nb, t, CONV_CH), lambda i, r: (i, r, acol + 1)),
                pl.BlockSpec((CONV_WIDTH, CONV_CH), lambda i, r: (0, 0)),
                pl.BlockSpec((1, CONV_CH), lambda i, r: (0, 0)),
                pl.BlockSpec((1, CONV_CH), lambda i, r: (0, 0)),
                pl.BlockSpec((1, CONV_CH), lambda i, r: (0, 0))]
    args = [z, z, conv_w, conv_b.reshape(1, -1), ln_g.reshape(1, -1), ln_b.reshape(1, -1)]
    if has_state:
        in_specs.append(pl.BlockSpec((nb, keep, CONV_CH), lambda i, r: (i, 0, 0)))
        args.append(state)
    return pl.pallas_call(
        functools.partial(_conformer_body, has_state=has_state),
        grid=(s // nb, tt // t),
        in_specs=in_specs,
        out_specs=[pl.BlockSpec((nb, t, CONV_CH), lambda i, r: (i, r, 0)),
                   pl.BlockSpec((nb, keep, CONV_CH), lambda i, r: (i, 0, 0))],
        out_shape=[jax.ShapeDtypeStruct((s, tt, CONV_CH), BF16),
                   jax.ShapeDtypeStruct((s, keep, CONV_CH), F32)],
        scratch_shapes=[pltpu.VMEM((nb, HIST_ROWS + t, CONV_CH), F32)],
        compiler_params=_params("arbitrary", "arbitrary"),
        name="conformer_conv",
    )(*args)


def _merge_body(ya_ref, yb_ref, yc_ref, w_ref, ga_ref, gb_ref, gc_ref, o_ref):
    nb, t, tn = o_ref.shape

    def branch(y_ref, i, g_ref):
        y = y_ref[...].reshape(nb * t, y_ref.shape[2])
        return jax.nn.sigmoid(g_ref[...]) * _dot(y, w_ref[i]).reshape(nb, t, tn)

    o_ref[...] = (branch(ya_ref, 0, ga_ref) + branch(yb_ref, 1, gb_ref)
                  + branch(yc_ref, 2, gc_ref)).astype(o_ref.dtype)


def gated_merge(ya, yb, yc, z, w_branch, nb, t, tn):
    s, tt, bw = ya.shape
    d = w_branch.shape[2]
    gcol = OFF_GA // tn
    per = d // tn
    yspec = pl.BlockSpec((nb, t, bw), lambda i, r, j: (i, r, 0))

    def gspec(which):
        return pl.BlockSpec((nb, t, tn), lambda i, r, j: (i, r, gcol + which * per + j))

    return pl.pallas_call(
        _merge_body,
        grid=(s // nb, tt // t, d // tn),
        in_specs=[yspec, yspec, yspec,
                  pl.BlockSpec((3, bw, tn), lambda i, r, j: (0, 0, j)),
                  gspec(0), gspec(1), gspec(2)],
        out_specs=pl.BlockSpec((nb, t, tn), lambda i, r, j: (i, r, j)),
        out_shape=jax.ShapeDtypeStruct((s, tt, d), BF16),
        compiler_params=_params("arbitrary", "arbitrary", "arbitrary"),
        name="gated_merge",
    )(ya, yb, yc, w_branch, z, z, z)


def _outproj_body(x_ref, m_ref, gate_ref, w_ref, lg_ref, lb_ref, o_ref, *, alpha):
    nb, t, d = x_ref.shape
    y = _dot(m_ref[...].reshape(nb * t, m_ref.shape[2]), w_ref[...]).reshape(nb, t, d)
    o_ref[...] = _layer_norm(alpha * x_ref[...] + (1.0 + gate_ref[...]) * y, lg_ref[...], lb_ref[...])


def outproj_norm(x, merged, ada, gate_chunk, w_o, ln_g, ln_b, alpha, nb, t):
    s, tt, d = x.shape
    tok = lambda i, r: (i, r, 0)
    return pl.pallas_call(
        functools.partial(_outproj_body, alpha=alpha),
        grid=(s // nb, tt // t),
        in_specs=[pl.BlockSpec((nb, t, d), tok),
                  pl.BlockSpec((nb, t, merged.shape[2]), tok),
                  pl.BlockSpec((nb, 1, d), lambda i, r: (i, 0, gate_chunk)),
                  pl.BlockSpec(w_o.shape, lambda i, r: (0, 0)),
                  pl.BlockSpec((1, d), lambda i, r: (0, 0)),
                  pl.BlockSpec((1, d), lambda i, r: (0, 0))],
        out_specs=pl.BlockSpec((nb, t, d), tok),
        out_shape=jax.ShapeDtypeStruct((s, tt, d), F32),
        compiler_params=_params("arbitrary", "arbitrary"),
        name="outproj_norm",
    )(x, merged, ada, w_o, ln_g.reshape(1, -1), ln_b.reshape(1, -1))


def _ffn_down_body(g_ref, u_ref, prev_ref, cw_ref, cb_ref, wd_ref, x_ref, gate_ref, lg_ref, lb_ref,
                   o_ref, gbuf, acc, *, alpha, zero_first):
    kk = pl.program_id(2)
    nb, t, tk = g_ref.shape
    keep = FFN_CONV_WIDTH - 1
    lead = FFN_HIST_ROWS - keep
    prev_rows = prev_ref.shape[1]

    prev = prev_ref[:, prev_rows - keep:prev_rows, :]
    if zero_first:
        prev = jnp.where(pl.program_id(1) == 0, 0.0, prev)
    gbuf[:, lead:FFN_HIST_ROWS, :] = prev
    gbuf[:, FFN_HIST_ROWS:FFN_HIST_ROWS + t, :] = g_ref[...]
    conv = jnp.zeros((nb, t, tk), F32) + cb_ref[...]
    for j in range(FFN_CONV_WIDTH):
        conv = conv + cw_ref[j:j + 1, :] * gbuf[:, lead + j:lead + j + t, :]
    act = (_silu(conv) * u_ref[...]).reshape(nb * t, tk).astype(BF16)

    @pl.when(kk == 0)
    def _():
        acc[...] = jnp.zeros_like(acc)

    acc[...] += _dot(act, wd_ref[...])

    @pl.when(kk == pl.num_programs(2) - 1)
    def _():
        y = acc[...].reshape(o_ref.shape)
        o_ref[...] = _layer_norm(alpha * x_ref[...] + (1.0 + gate_ref[...]) * y, lg_ref[...], lb_ref[...])


def ffn_down_norm(gu, prev, x, ada, gate_chunk, conv_w, conv_b, w_down, ln_g, ln_b, alpha, nb, t, tk,
                  prev_is_state):
    s, tt, f2 = gu.shape
    f = f2 // 2
    d = x.shape[2]
    nk = f // tk
    if prev_is_state:
        assert tt == t
        prev_spec = pl.BlockSpec((nb, prev.shape[1], tk), lambda i, r, k: (i, 0, k))
    else:
        assert nb == 1
        per = t // FFN_HIST_ROWS
        prev_spec = pl.BlockSpec((nb, FFN_HIST_ROWS, tk),
                                 lambda i, r, k: (i, jnp.maximum(r * per - 1, 0), k))
    tok = lambda i, r, k: (i, r, 0)
    return pl.pallas_call(
        functools.partial(_ffn_down_body, alpha=alpha, zero_first=not prev_is_state),
        grid=(s // nb, tt // t, nk),
        in_specs=[pl.BlockSpec((nb, t, tk), lambda i, r, k: (i, r, k)),
                  pl.BlockSpec((nb, t, tk), lambda i, r, k: (i, r, nk + k)),
                  prev_spec,
                  pl.BlockSpec((FFN_CONV_WIDTH, tk), lambda i, r, k: (0, k)),
                  pl.BlockSpec((1, tk), lambda i, r, k: (0, k)),
                  pl.BlockSpec((tk, d), lambda i, r, k: (k, 0)),
                  pl.BlockSpec((nb, t, d), tok),
                  pl.BlockSpec((nb, 1, d), lambda i, r, k: (i, 0, gate_chunk)),
                  pl.BlockSpec((1, d), lambda i, r, k: (0, 0)),
                  pl.BlockSpec((1, d), lambda i, r, k: (0, 0))],
        out_specs=pl.BlockSpec((nb, t, d), tok),
        out_shape=jax.ShapeDtypeStruct((s, tt, d), F32),
        scratch_shapes=[pltpu.VMEM((nb, FFN_HIST_ROWS + t, tk), F32),
                        pltpu.VMEM((nb * t, d), F32)],
        compiler_params=_params("arbitrary", "arbitrary", "arbitrary"),
        name="ffn_down_norm",
    )(gu, gu, prev, conv_w, conv_b.reshape(1, -1), w_down, x, ada, ln_g.reshape(1, -1), ln_b.reshape(1, -1))


def _tile(s, t, rows):
    if t >= rows:
        assert t % rows == 0
        return 1, rows
    nb = max(1, min(s, rows // t))
    assert s % nb == 0
    return nb, t


def trunk_layer(x, ada, attend, conv_state, ret_state, ffn_state, pos0, alpha, p):
    s, t, d = x.shape
    nb, tr = _tile(s, t, 1024 if t >= 1024 else 512)
    z = modulated_matmul(x, ada, 1, 0, p["w_in"], nb, tr, 512)
    ya = attend(z)
    yb, ret_new = retention(z, p["ret_gn_g"], ret_state, pos0)
    nbc, trc = _tile(s, t, 256) if t >= 256 else _tile(s, t, 128)
    yc, conv_new = conformer_conv(z, p["conv_w"], p["conv_b"], p["conv_ln_g"], p["conv_ln_b"],
                                  conv_state, nbc, trc)
    nbm, trm = _tile(s, t, 512)
    merged = gated_merge(ya, yb, yc, z, p["w_branch"], nbm, trm, 512)
    x1 = outproj_norm(x, merged, ada, 2, p["w_o"], p["ln1_g"], p["ln1_b"], alpha, nbm, trm)
    gu = modulated_matmul(x1, ada, 4, 3, p["w_up"], nb, tr, 512)
    f = gu.shape[2] // 2
    prev = gu if ffn_state is None else ffn_state
    x2 = ffn_down_norm(gu, prev, x1, ada, 5, p["ffn_conv_w"], p["ffn_conv_b"], p["w_down"],
                       p["ln2_g"], p["ln2_b"], alpha, nbm, trm, 512, ffn_state is not None)
    k_new = z[:, :, OFF_AK:OFF_AK + ATT_W].reshape(s, t, ATT_HEADS, ATT_DIM)
    v_new = z[:, :, OFF_AV:OFF_AV + ATT_W].reshape(s, t, ATT_HEADS, ATT_DIM)
    ffn_new = gu[:, t - (FFN_CONV_WIDTH - 1):, :f]
    return x2, k_new, v_new, conv_new, ret_new, ffn_new


def kernel(x_prompt, x_sample, cache_k, cache_v, state_conv, state_ret, state_ffn, page_table,
           c_prompt, c_sample, w_ada, b_ada, w_in, w_branch, w_o, ln1_g, ln1_b, ret_gn_g,
           conv_w, conv_b, conv_ln_g, conv_ln_b, w_up, ffn_conv_w, ffn_conv_b, w_down, ln2_g, ln2_b):
    depth = w_ada.shape[0]
    b = x_prompt.shape[0]
    sb = x_sample.shape[0]
    alpha = (2 * depth) ** 0.25
    past = page_table.shape[1] * PAGE_SIZE

    c_all = jnp.concatenate([c_prompt, c_sample], axis=0)
    pad = (-c_all.shape[0]) % 8
    ada = ada_all(jnp.pad(c_all, ((0, pad), (0, 0))), w_ada, b_ada)
    ada_p = ada[:, :b, None, :]
    ada_s = ada[:, b:b + sb, None, :]

    xp, xs = x_prompt, x_sample
    outs_p, outs_s = [], []
    for l in range(depth):
        p = {"w_in": w_in[l].astype(BF16), "w_branch": w_branch[l].astype(BF16), "w_o": w_o[l].astype(BF16),
             "w_up": w_up[l].astype(BF16), "w_down": w_down[l].astype(BF16),
             "ln1_g": ln1_g[l], "ln1_b": ln1_b[l], "ret_gn_g": ret_gn_g[l], "conv_w": conv_w[l],
             "conv_b": conv_b[l], "conv_ln_g": conv_ln_g[l], "conv_ln_b": conv_ln_b[l],
             "ffn_conv_w": ffn_conv_w[l], "ffn_conv_b": ffn_conv_b[l], "ln2_g": ln2_g[l], "ln2_b": ln2_b[l]}
        xp, *rest_p = trunk_layer(xp, ada_p[l], moba_prompt, None, None, None, 0, alpha, p)
        attend_s = functools.partial(moba_sample, cache_k=cache_k, cache_v=cache_v,
                                     page_table=page_table, layer=l)
        xs, *rest_s = trunk_layer(xs, ada_s[l], attend_s, state_conv[l], state_ret[l], state_ffn[l],
                                  past, alpha, p)
        outs_p.append(rest_p)
        outs_s.append(rest_s)
    stack = lambda outs, i: jnp.stack([o[i] for o in outs])
    return (xp, xs, *(stack(outs_p, i) for i in range(5)), *(stack(outs_s, i) for i in range(5)))
```

```python
import functools
import math

import jax
import jax.numpy as jnp
from jax import lax
from jax.experimental import pallas as pl
from jax.experimental.pallas import tpu as pltpu

F32 = jnp.float32
BF16 = jnp.bfloat16

PAGE_SIZE = 128
ATT_HEADS = 8
ATT_DIM = 128
ATT_W = ATT_HEADS * ATT_DIM
MOBA_BLOCK = 256
MOBA_TOPK = 3
MOBA_CHUNK_BLOCKS = 4
MOBA_HEADS_PER_STEP = 2
RET_HEADS = 4
RET_DK = 256
RET_DV = 256
RET_CHUNK = 128
CONV_CH = 1024
CONV_WIDTH = 31
FFN_CONV_WIDTH = 3
LN_EPS = 1e-5
NEG_BIG = -1e30

OFF_AQ, OFF_AK, OFF_AV = 0, ATT_W, 2 * ATT_W
OFF_BQ = 3 * ATT_W
OFF_BK = OFF_BQ + RET_HEADS * RET_DK
OFF_BV = OFF_BK + RET_HEADS * RET_DK
OFF_BG = OFF_BV + RET_HEADS * RET_DV
OFF_C = OFF_BG + RET_HEADS * RET_DV
OFF_GA = OFF_C + 2 * CONV_CH

VMEM_LIMIT_BYTES = 56 * 2**20
HIST_ROWS = 32
FFN_HIST_ROWS = 8


def _params(*sem):
    return pltpu.CompilerParams(dimension_semantics=sem, vmem_limit_bytes=VMEM_LIMIT_BYTES)


def _dot(a, b):
    return jnp.dot(a, b, preferred_element_type=F32)


def _dot_nt(a, b, precision=None):
    return lax.dot_general(a, b, (((1,), (1,)), ((), ())), precision=precision,
                           preferred_element_type=F32)


def _dot_tn(a, b):
    return lax.dot_general(a, b, (((0,), (0,)), ((), ())), preferred_element_type=F32)


def _silu(x):
    return x * jax.nn.sigmoid(x)


def _layer_norm(x, g, b):
    mu = jnp.mean(x, -1, keepdims=True)
    xc = x - mu
    var = jnp.mean(xc * xc, -1, keepdims=True)
    return xc * lax.rsqrt(var + LN_EPS) * g + b


def _ada_body(c_ref, w_ref, b_ref, o_ref):
    a = _silu(c_ref[...]).astype(BF16)
    o_ref[...] = _dot(a, w_ref[...].astype(BF16)) + b_ref[...]


def ada_all(c_all, w_ada, b_ada, tn=1024):
    depth, d, n = w_ada.shape
    r = c_all.shape[0]
    return pl.pallas_call(
        _ada_body,
        grid=(depth, n // tn),
        in_specs=[pl.BlockSpec((r, d), lambda l, j: (0, 0)),
                  pl.BlockSpec((None, d, tn), lambda l, j: (l, 0, j)),
                  pl.BlockSpec((None, 1, tn), lambda l, j: (l, 0, j))],
        out_specs=pl.BlockSpec((None, r, tn), lambda l, j: (l, 0, j)),
        out_shape=jax.ShapeDtypeStruct((depth, r, n), F32),
        compiler_params=_params("arbitrary", "arbitrary"),
        name="ada",
    )(c_all, w_ada, b_ada.reshape(depth, 1, n))


def _modmm_body(x_ref, sc_ref, sh_ref, w_ref, o_ref, h_ref):
    @pl.when(pl.program_id(2) == 0)
    def _():
        h = x_ref[...] * (1.0 + sc_ref[...]) + sh_ref[...]
        h_ref[...] = h.reshape(h_ref.shape).astype(BF16)

    o_ref[...] = _dot(h_ref[...], w_ref[...]).reshape(o_ref.shape)


def modulated_matmul(x, ada, sc_chunk, sh_chunk, w, nb, t, tn):
    s, tt, d = x.shape
    n = w.shape[1]
    return pl.pallas_call(
        _modmm_body,
        grid=(s // nb, tt // t, n // tn),
        in_specs=[pl.BlockSpec((nb, t, d), lambda i, r, j: (i, r, 0)),
                  pl.BlockSpec((nb, 1, d), lambda i, r, j: (i, 0, sc_chunk)),
                  pl.BlockSpec((nb, 1, d), lambda i, r, j: (i, 0, sh_chunk)),
                  pl.BlockSpec((d, tn), lambda i, r, j: (0, j))],
        out_specs=pl.BlockSpec((nb, t, tn), lambda i, r, j: (i, r, j)),
        out_shape=jax.ShapeDtypeStruct((s, tt, n), F32),
        scratch_shapes=[pltpu.VMEM((nb * t, d), BF16)],
        compiler_params=_params("arbitrary", "arbitrary", "arbitrary"),
        name="modmm",
    )(x, ada, ada, w)


def _select_blocks(score, n_past, n_blocks, axis):
    blk = lax.broadcasted_iota(jnp.int32, score.shape, axis)
    past = blk < n_past
    s = jnp.where(past, score, -jnp.inf)
    rank = jnp.zeros(score.shape, jnp.int32)
    for m in range(n_blocks):
        sm = s[m:m + 1, :] if axis == 0 else s[:, m:m + 1]
        beats = jnp.where(sm > s, 1, jnp.where(sm == s, jnp.where(blk > m, 1, 0), 0))
        rank = rank + beats
    return jnp.where(past, jnp.where(rank < MOBA_TOPK, 1.0, 0.0), 0.0)


def _moba_prompt_body(q_ref, k_ref, v_ref, o_ref, kb_ref, vt_ref, km_ref, m_scr, l_scr, acc_scr):
    qi = pl.program_id(2)
    n_blocks = k_ref.shape[0] // MOBA_BLOCK
    scale = ATT_DIM ** -0.5
    heads = [slice(g * ATT_DIM, (g + 1) * ATT_DIM) for g in range(MOBA_HEADS_PER_STEP)]

    @pl.when(qi == 0)
    def _():
        for g, hd in enumerate(heads):
            kb_ref[g] = k_ref[:, hd].astype(BF16)
            for n in range(n_blocks):
                rows = slice(n * MOBA_BLOCK, (n + 1) * MOBA_BLOCK)
                vt_ref[g, n] = v_ref[rows, hd].T.astype(BF16)
                km_ref[g, n:n + 1, :] = jnp.mean(k_ref[rows, hd], axis=0, keepdims=True)

    start = pl.multiple_of(qi * MOBA_BLOCK, MOBA_BLOCK)
    key = lax.broadcasted_iota(jnp.int32, (MOBA_BLOCK, MOBA_BLOCK), 0)
    qry = lax.broadcasted_iota(jnp.int32, (MOBA_BLOCK, MOBA_BLOCK), 1)
    qbs, biases = [], []
    for g, hd in enumerate(heads):
        q = q_ref[:, hd]
        qb = q.astype(BF16)
        score = _dot_nt(km_ref[g], q, precision=lax.Precision.HIGHEST)
        biases.append((_select_blocks(score, qi, n_blocks, 0) - 1.0) * (-NEG_BIG))
        qbs.append(qb)
        s0 = _dot_nt(kb_ref[g, pl.ds(start, MOBA_BLOCK), :], qb) * scale
        s0 = jnp.where(key <= qry, s0, NEG_BIG)
        m0 = jnp.max(s0, axis=0, keepdims=True)
        p0 = jnp.exp(s0 - m0)
        m_scr[g] = m0
        l_scr[g] = jnp.sum(p0, axis=0, keepdims=True)
        acc_scr[g] = _dot(vt_ref[g, qi], p0.astype(BF16))

    span = MOBA_CHUNK_BLOCKS * MOBA_BLOCK
    for c in range(n_blocks // MOBA_CHUNK_BLOCKS):
        @pl.when(c * MOBA_CHUNK_BLOCKS < qi)
        def _():
            first = c * MOBA_CHUNK_BLOCKS
            for g in range(MOBA_HEADS_PER_STEP):
                s = _dot_nt(kb_ref[g, c * span:(c + 1) * span, :], qbs[g]) * scale
                s = jnp.concatenate(
                    [s[j * MOBA_BLOCK:(j + 1) * MOBA_BLOCK, :] + biases[g][first + j:first + j + 1, :]
                     for j in range(MOBA_CHUNK_BLOCKS)], axis=0)
                m = m_scr[g]
                m_new = jnp.maximum(m, jnp.max(s, axis=0, keepdims=True))
                alpha = jnp.exp(m - m_new)
                p = jnp.exp(s - m_new)
                l_scr[g] = alpha * l_scr[g] + jnp.sum(p, axis=0, keepdims=True)
                pb = p.astype(BF16)
                pv = _dot(vt_ref[g, first], pb[0:MOBA_BLOCK, :])
                for j in range(1, MOBA_CHUNK_BLOCKS):
                    pv = pv + _dot(vt_ref[g, first + j], pb[j * MOBA_BLOCK:(j + 1) * MOBA_BLOCK, :])
                acc_scr[g] = alpha * acc_scr[g] + pv
                m_scr[g] = m_new

    for g, hd in enumerate(heads):
        o_ref[:, hd] = (acc_scr[g] / l_scr[g]).T.astype(o_ref.dtype)


def moba_prompt(z):
    b, t, _ = z.shape
    assert t % (MOBA_BLOCK * MOBA_CHUNK_BLOCKS) == 0
    nq = t // MOBA_BLOCK
    assert nq % 8 == 0 and ATT_HEADS % MOBA_HEADS_PER_STEP == 0
    hps = MOBA_HEADS_PER_STEP
    w = hps * ATT_DIM
    kcol = OFF_AK // w
    vcol = OFF_AV // w
    return pl.pallas_call(
        _moba_prompt_body,
        grid=(b, ATT_HEADS // hps, nq),
        in_specs=[pl.BlockSpec((None, MOBA_BLOCK, w), lambda i, h, q: (i, q, h)),
                  pl.BlockSpec((None, t, w), lambda i, h, q: (i, 0, kcol + h)),
                  pl.BlockSpec((None, t, w), lambda i, h, q: (i, 0, vcol + h))],
        out_specs=pl.BlockSpec((None, MOBA_BLOCK, w), lambda i, h, q: (i, q, h)),
        out_shape=jax.ShapeDtypeStruct((b, t, ATT_W), BF16),
        scratch_shapes=[pltpu.VMEM((hps, t, ATT_DIM), BF16),
                        pltpu.VMEM((hps, nq, ATT_DIM, MOBA_BLOCK), BF16),
                        pltpu.VMEM((hps, nq, ATT_DIM), F32),
                        pltpu.VMEM((hps, 1, MOBA_BLOCK), F32), pltpu.VMEM((hps, 1, MOBA_BLOCK), F32),
                        pltpu.VMEM((hps, ATT_DIM, MOBA_BLOCK), F32)],
        compiler_params=_params("arbitrary", "arbitrary", "arbitrary"),
        name="moba_prompt",
    )(z, z, z)


def _softmax_partial(qb, kb, vb, bias):
    s = _dot_nt(qb, kb) * (ATT_DIM ** -0.5) + bias
    m = jnp.max(s, axis=1, keepdims=True)
    p = jnp.exp(s - m)
    return m, jnp.sum(p, axis=1, keepdims=True), _dot(p.astype(BF16), vb)


def _moba_sample_body(pt_ref, q_ref, kn_ref, vn_ref, *refs):
    del pt_ref
    n_pages = (len(refs) - 4) // 2
    k_refs, v_refs = refs[:n_pages], refs[n_pages:2 * n_pages]
    o_ref, km_ref, ko_ref, vo_ref = refs[2 * n_pages:]
    pages_per_block = MOBA_BLOCK // PAGE_SIZE
    n_blocks = n_pages // pages_per_block
    rows = q_ref.shape[0]
    hr = ATT_HEADS * rows
    pad_rows = ko_ref.shape[0]

    def head_major(x):
        return jnp.concatenate([x[:, h * ATT_DIM:(h + 1) * ATT_DIM] for h in range(ATT_HEADS)], axis=0)

    q = head_major(q_ref[...])
    qb = q.astype(BF16)

    blk_rows = MOBA_BLOCK * ATT_HEADS
    q_head = lax.broadcasted_iota(jnp.int32, (hr, blk_rows), 0) // rows
    k_head = lax.broadcasted_iota(jnp.int32, (hr, blk_rows), 1) % ATT_HEADS
    head_bias = jnp.where(q_head == k_head, 0.0, NEG_BIG)
    km_ref[...] = jnp.zeros_like(km_ref)
    parts = []
    for n in range(n_blocks):
        pages = range(n * pages_per_block, (n + 1) * pages_per_block)
        kblk = jnp.concatenate([k_refs[pg][...] for pg in pages], axis=0)
        vblk = jnp.concatenate([v_refs[pg][...] for pg in pages], axis=0)
        parts.append(_softmax_partial(qb, kblk.astype(BF16), vblk.astype(BF16), head_bias))
        km_ref[n * ATT_HEADS:(n + 1) * ATT_HEADS, :] = (
            jnp.sum(kblk.reshape(MOBA_BLOCK, ATT_HEADS, ATT_DIM), axis=0) * (1.0 / MOBA_BLOCK))

    sc = _dot_nt(q, km_ref[...], precision=lax.Precision.HIGHEST)
    col = lax.broadcasted_iota(jnp.int32, sc.shape, 1)
    row_head = lax.broadcasted_iota(jnp.int32, sc.shape, 0) // rows
    sc = jnp.where(col % ATT_HEADS == row_head, jnp.where(col < n_blocks * ATT_HEADS, sc, 0.0), 0.0)
    g_row = lax.broadcasted_iota(jnp.int32, (128, 128), 0)
    g_col = lax.broadcasted_iota(jnp.int32, (128, 128), 1)
    gather = jnp.where(g_row // ATT_HEADS == g_col, 1.0, 0.0)
    score = jnp.dot(sc, gather, precision=lax.Precision.HIGHEST, preferred_element_type=F32)
    sel = _select_blocks(score, n_blocks, n_blocks, 1)

    ko_ref[...] = jnp.zeros_like(ko_ref)
    vo_ref[...] = jnp.zeros_like(vo_ref)
    ko_ref[0:hr, :] = head_major(kn_ref[...]).astype(BF16)
    vo_ref[0:hr, :] = head_major(vn_ref[...]).astype(BF16)
    o_qrow = lax.broadcasted_iota(jnp.int32, (hr, pad_rows), 0)
    o_kcol = lax.broadcasted_iota(jnp.int32, (hr, pad_rows), 1)
    own_bias = jnp.where(o_kcol // rows == o_qrow // rows,
                         jnp.where(o_kcol % rows <= o_qrow % rows, 0.0, NEG_BIG), NEG_BIG)
    m_o, l_o, acc_o = _softmax_partial(qb, ko_ref[...], vo_ref[...], own_bias)

    m_tot = m_o
    for n in range(n_blocks):
        m_tot = jnp.maximum(m_tot, jnp.where(sel[:, n:n + 1] > 0.0, parts[n][0], NEG_BIG))
    w_o = jnp.exp(m_o - m_tot)
    num = w_o * acc_o
    den = w_o * l_o
    for n in range(n_blocks):
        m_n, l_n, acc_n = parts[n]
        w = jnp.where(sel[:, n:n + 1] > 0.0, jnp.exp(m_n - m_tot), 0.0)
        num = num + w * acc_n
        den = den + w * l_n
    out = num / den
    for h in range(ATT_HEADS):
        o_ref[:, h * ATT_DIM:(h + 1) * ATT_DIM] = out[h * rows:(h + 1) * rows, :].astype(o_ref.dtype)


def moba_sample(z, cache_k, cache_v, page_table, layer):
    s, rows, _ = z.shape
    n_pages = page_table.shape[1]
    pages_per_block = MOBA_BLOCK // PAGE_SIZE
    assert MOBA_BLOCK % PAGE_SIZE == 0 and n_pages % pages_per_block == 0
    n_blocks = n_pages // pages_per_block
    hr = ATT_HEADS * rows
    assert hr <= 128 and n_blocks * ATT_HEADS <= 128
    depth, n_phys = cache_k.shape[0], cache_k.shape[1]
    ck = cache_k.reshape(depth, n_phys, PAGE_SIZE * ATT_HEADS, ATT_DIM)
    cv = cache_v.reshape(depth, n_phys, PAGE_SIZE * ATT_HEADS, ATT_DIM)

    def page_spec(which):
        return pl.BlockSpec((None, None, PAGE_SIZE * ATT_HEADS, ATT_DIM),
                            lambda i, pt: (layer, pt[i, which], 0, 0))

    def z_spec(col):
        return pl.BlockSpec((None, rows, ATT_W), lambda i, pt: (i, 0, col))

    page_specs = [page_spec(pg) for pg in range(n_pages)]
    grid_spec = pltpu.PrefetchScalarGridSpec(
        num_scalar_prefetch=1,
        grid=(s,),
        in_specs=[z_spec(OFF_AQ // ATT_W), z_spec(OFF_AK // ATT_W), z_spec(OFF_AV // ATT_W)]
        + page_specs + page_specs,
        out_specs=pl.BlockSpec((None, rows, ATT_W), lambda i, pt: (i, 0, 0)),
        scratch_shapes=[pltpu.VMEM((128, ATT_DIM), F32),
                        pltpu.VMEM((128, ATT_DIM), BF16),
                        pltpu.VMEM((128, ATT_DIM), BF16)])
    return pl.pallas_call(
        _moba_sample_body,
        grid_spec=grid_spec,
        out_shape=jax.ShapeDtypeStruct((s, rows, ATT_W), BF16),
        compiler_params=_params("arbitrary"),
        name="moba_sample",
    )(page_table, z, z, z, *([ck] * n_pages), *([cv] * n_pages))


def _retention_body(*refs, rows, has_state):
    if has_state:
        (q_ref, k_ref, v_ref, g_ref, cos_ref, sin_ref, intra_ref, qd_ref, kd_ref, cd_ref, gn_ref,
         r0_ref, o_ref, rout_ref, r_scr) = refs
    else:
        (q_ref, k_ref, v_ref, g_ref, cos_ref, sin_ref, intra_ref, qd_ref, kd_ref, cd_ref, gn_ref,
         o_ref, rout_ref, r_scr) = refs
    ci = pl.program_id(1)
    c = intra_ref.shape[1]

    @pl.when(ci == 0)
    def _():
        if has_state:
            r_scr[...] = r0_ref[...]
        else:
            r_scr[...] = jnp.zeros_like(r_scr)

    def pad(x):
        if rows == c:
            return x
        return jnp.concatenate([x, jnp.zeros((c - rows, x.shape[1]), x.dtype)], axis=0)

    cos = pad(cos_ref[...])
    sin = pad(sin_ref[...])
    even = lax.broadcasted_iota(jnp.int32, cos.shape, 1) % 2 == 0

    def rotate(x):
        nxt = pltpu.roll(x, x.shape[1] - 1, axis=1)
        prv = pltpu.roll(x, 1, axis=1)
        return x * cos + jnp.where(even, nxt, prv) * sin

    for h in range(RET_HEADS):
        qk = slice(h * RET_DK, (h + 1) * RET_DK)
        vv = slice(h * RET_DV, (h + 1) * RET_DV)
        q = rotate(pad(q_ref[:, qk]))
        k = rotate(pad(k_ref[:, qk])) * (RET_DK ** -0.5)
        v = pad(v_ref[:, vv])
        r = r_scr[h]
        qb, kb, vb = q.astype(BF16), k.astype(BF16), v.astype(BF16)
        s = _dot_nt(qb, kb) * intra_ref[h]
        o = _dot(s.astype(BF16), vb) + _dot(qb, r.astype(BF16)) * qd_ref[h]
        r_new = r * cd_ref[h] + _dot_tn((k * kd_ref[h]).astype(BF16), vb)
        r_scr[h] = r_new

        mu = jnp.mean(o, -1, keepdims=True)
        oc = o - mu
        var = jnp.mean(oc * oc, -1, keepdims=True)
        on = oc * lax.rsqrt(var + LN_EPS) * gn_ref[:, vv]
        y = _silu(pad(g_ref[:, vv])) * on
        o_ref[:, vv] = y[0:rows, :].astype(o_ref.dtype)

    @pl.when(ci == pl.num_programs(1) - 1)
    def _():
        rout_ref[...] = r_scr[...]


def _retention_tables(t, pos0, c_true, c_pad):
    inv = 1.0 / (10000.0 ** jnp.linspace(0.0, 1.0, RET_DK // 2, dtype=F32))
    pos = pos0 + jnp.arange(t, dtype=jnp.int32)
    ang = pos.astype(F32)[:, None] * inv[None, :]
    cos = jnp.repeat(jnp.cos(ang), 2, axis=-1)
    sin = jnp.stack([-jnp.sin(ang), jnp.sin(ang)], axis=-1).reshape(t, RET_DK)
    log_g = jnp.log1p(-jnp.exp2(-5.0 - jnp.arange(RET_HEADS, dtype=F32)))
    i = jnp.arange(c_pad, dtype=F32)
    live = i < c_true
    diff = i[:, None] - i[None, :]
    intra = jnp.where(diff >= 0, jnp.exp(log_g[:, None, None] * jnp.maximum(diff, 0.0)), 0.0)
    intra = jnp.where(live[None, :, None] & live[None, None, :], intra, 0.0)
    q_decay = jnp.exp(log_g[:, None] * (i[None, :] + 1.0))
    k_decay = jnp.where(live[None, :], jnp.exp(log_g[:, None] * (c_true - 1.0 - i[None, :])), 0.0)
    chunk_decay = jnp.exp(log_g * c_true)
    return (cos, sin, intra, q_decay[:, :, None], k_decay[:, :, None], chunk_decay[:, None, None])


def retention(z, gn_g, state, pos0):
    s, t, _ = z.shape
    c_true = math.gcd(RET_CHUNK, t)
    c_pad = RET_CHUNK
    n_chunks = t // c_true
    cos, sin, intra, qd, kd, cd = _retention_tables(t, pos0, c_true, c_pad)
    has_state = state is not None
    qk_w, v_w = RET_HEADS * RET_DK, RET_HEADS * RET_DV

    def zspec(off, width):
        return pl.BlockSpec((None, c_true, width), lambda i, ci: (i, ci, off // width))

    whole = lambda a: pl.BlockSpec(a.shape, lambda i, ci: (0,) * a.ndim)
    state_spec = pl.BlockSpec((None, RET_HEADS, RET_DK, RET_DV), lambda i, ci: (i, 0, 0, 0))
    gn = gn_g.reshape(1, -1)
    in_specs = [zspec(OFF_BQ, qk_w), zspec(OFF_BK, qk_w), zspec(OFF_BV, v_w), zspec(OFF_BG, v_w),
                pl.BlockSpec((c_true, RET_DK), lambda i, ci: (ci, 0)),
                pl.BlockSpec((c_true, RET_DK), lambda i, ci: (ci, 0)),
                whole(intra), whole(qd), whole(kd), whole(cd), whole(gn)]
    args = [z, z, z, z, cos, sin, intra, qd, kd, cd, gn]
    if has_state:
        in_specs.append(state_spec)
        args.append(state)
    return pl.pallas_call(
        functools.partial(_retention_body, rows=c_true, has_state=has_state),
        grid=(s, n_chunks),
        in_specs=in_specs,
        out_specs=[pl.BlockSpec((None, c_true, v_w), lambda i, ci: (i, ci, 0)), state_spec],
        out_shape=[jax.ShapeDtypeStruct((s, t, v_w), BF16),
                   jax.ShapeDtypeStruct((s, RET_HEADS, RET_DK, RET_DV), F32)],
        scratch_shapes=[pltpu.VMEM((RET_HEADS, RET_DK, RET_DV), F32)],
        compiler_params=_params("arbitrary", "arbitrary"),
        name="retention",
    )(*args)


def _conformer_body(*refs, has_state):
    if has_state:
        a_ref, g_ref, w_ref, b_ref, lg_ref, lb_ref, st_ref, o_ref, cn_ref, buf = refs
    else:
        a_ref, g_ref, w_ref, b_ref, lg_ref, lb_ref, o_ref, cn_ref, buf = refs
    ti = pl.program_id(1)
    nb, t, ch = a_ref.shape
    keep = CONV_WIDTH - 1
    lead = HIST_ROWS - keep

    @pl.when(ti == 0)
    def _():
        buf[:, 0:HIST_ROWS, :] = jnp.zeros((nb, HIST_ROWS, ch), F32)
        if has_state:
            buf[:, lead:HIST_ROWS, :] = st_ref[...]

    buf[:, HIST_ROWS:HIST_ROWS + t, :] = a_ref[...] * jax.nn.sigmoid(g_ref[...])
    acc = jnp.zeros((nb, t, ch), F32) + b_ref[...]
    for shift in range(8):
        taps = [j for j in range(CONV_WIDTH) if (lead + j) % 8 == shift]
        span = t + (8 if shift else 0)
        part = None
        for j in taps:
            base = lead + j - shift
            term = w_ref[j:j + 1, :] * buf[:, base:base + span, :]
            part = term if part is None else part + term
        if part is not None:
            acc = acc + part[:, shift:shift + t, :]
    y = _silu(_layer_norm(acc, lg_ref[...], lb_ref[...]))
    o_ref[...] = y.astype(o_ref.dtype)

    hist = buf[:, t:t + HIST_ROWS, :]
    buf[:, 0:HIST_ROWS, :] = hist

    @pl.when(ti == pl.num_programs(1) - 1)
    def _():
        cn_ref[...] = hist[:, lead:HIST_ROWS, :]


def conformer_conv(z, conv_w, conv_b, ln_g, ln_b, state, nb, t):
    s, tt, _ = z.shape
    keep = CONV_WIDTH - 1
    has_state = state is not None
    acol = OFF_C // CONV_CH
    in_specs = [pl.BlockSpec((nb, t, CONV_CH), lambda i, r: (i, r, acol)),
                pl.BlockSpec((nb, t, CONV_CH), lambda i, r: (i, r, acol + 1)),
                pl.BlockSpec((CONV_WIDTH, CONV_CH), lambda i, r: (0, 0)),
                pl.BlockSpec((1, CONV_CH), lambda i, r: (0, 0)),
                pl.BlockSpec((1, CONV_CH), lambda i, r: (0, 0)),
                pl.BlockSpec((1, CONV_CH), lambda i, r: (0, 0))]
    args = [z, z, conv_w, conv_b.reshape(1, -1), ln_g.reshape(1, -1), ln_b.reshape(1, -1)]
    if has_state:
        in_specs.append(pl.BlockSpec((nb, keep, CONV_CH), lambda i, r: (i, 0, 0)))
        args.append(state)
    return pl.pallas_call(
        functools.partial(_conformer_body, has_state=has_state),
        grid=(s // nb, tt // t),
        in_specs=in_specs,
        out_specs=[pl.BlockSpec((nb, t, CONV_CH), lambda i, r: (i, r, 0)),
                   pl.BlockSpec((nb, keep, CONV_CH), lambda i, r: (i, 0, 0))],
        out_shape=[jax.ShapeDtypeStruct((s, tt, CONV_CH), BF16),
                   jax.ShapeDtypeStruct((s, keep, CONV_CH), F32)],
        scratch_shapes=[pltpu.VMEM((nb, HIST_ROWS + t, CONV_CH), F32)],
        compiler_params=_params("arbitrary", "arbitrary"),
        name="conformer_conv",
    )(*args)


def _merge_body(ya_ref, yb_ref, yc_ref, w_ref, ga_ref, gb_ref, gc_ref, o_ref):
    nb, t, tn = o_ref.shape

    def branch(y_ref, i, g_ref):
        y = y_ref[...].reshape(nb * t, y_ref.shape[2])
        return jax.nn.sigmoid(g_ref[...]) * _dot(y, w_ref[i]).reshape(nb, t, tn)

    o_ref[...] = (branch(ya_ref, 0, ga_ref) + branch(yb_ref, 1, gb_ref)
                  + branch(yc_ref, 2, gc_ref)).astype(o_ref.dtype)


def gated_merge(ya, yb, yc, z, w_branch, nb, t, tn):
    s, tt, bw = ya.shape
    d = w_branch.shape[2]
    gcol = OFF_GA // tn
    per = d // tn
    yspec = pl.BlockSpec((nb, t, bw), lambda i, r, j: (i, r, 0))

    def gspec(which):
        return pl.BlockSpec((nb, t, tn), lambda i, r, j: (i, r, gcol + which * per + j))

    return pl.pallas_call(
        _merge_body,
        grid=(s // nb, tt // t, d // tn),
        in_specs=[yspec, yspec, yspec,
                  pl.BlockSpec((3, bw, tn), lambda i, r, j: (0, 0, j)),
                  gspec(0), gspec(1), gspec(2)],
        out_specs=pl.BlockSpec((nb, t, tn), lambda i, r, j: (i, r, j)),
        out_shape=jax.ShapeDtypeStruct((s, tt, d), BF16),
        compiler_params=_params("arbitrary", "arbitrary", "arbitrary"),
        name="gated_merge",
    )(ya, yb, yc, w_branch, z, z, z)


def _outproj_body(x_ref, m_ref, gate_ref, w_ref, lg_ref, lb_ref, o_ref, *, alpha):
    nb, t, d = x_ref.shape
    y = _dot(m_ref[...].reshape(nb * t, m_ref.shape[2]), w_ref[...]).reshape(nb, t, d)
    o_ref[...] = _layer_norm(alpha * x_ref[...] + (1.0 + gate_ref[...]) * y, lg_ref[...], lb_ref[...])


def outproj_norm(x, merged, ada, gate_chunk, w_o, ln_g, ln_b, alpha, nb, t):
    s, tt, d = x.shape
    tok = lambda i, r: (i, r, 0)
    return pl.pallas_call(
        functools.partial(_outproj_body, alpha=alpha),
        grid=(s // nb, tt // t),
        in_specs=[pl.BlockSpec((nb, t, d), tok),
                  pl.BlockSpec((nb, t, merged.shape[2]), tok),
                  pl.BlockSpec((nb, 1, d), lambda i, r: (i, 0, gate_chunk)),
                  pl.BlockSpec(w_o.shape, lambda i, r: (0, 0)),
                  pl.BlockSpec((1, d), lambda i, r: (0, 0)),
                  pl.BlockSpec((1, d), lambda i, r: (0, 0))],
        out_specs=pl.BlockSpec((nb, t, d), tok),
        out_shape=jax.ShapeDtypeStruct((s, tt, d), F32),
        compiler_params=_params("arbitrary", "arbitrary"),
        name="outproj_norm",
    )(x, merged, ada, w_o, ln_g.reshape(1, -1), ln_b.reshape(1, -1))


def _ffn_down_body(g_ref, u_ref, prev_ref, cw_ref, cb_ref, wd_ref, x_ref, gate_ref, lg_ref, lb_ref,
                   o_ref, gbuf, acc, *, alpha, zero_first):
    kk = pl.program_id(2)
    nb, t, tk = g_ref.shape
    keep = FFN_CONV_WIDTH - 1
    lead = FFN_HIST_ROWS - keep
    prev_rows = prev_ref.shape[1]

    prev = prev_ref[:, prev_rows - keep:prev_rows, :]
    if zero_first:
        prev = jnp.where(pl.program_id(1) == 0, 0.0, prev)
    gbuf[:, lead:FFN_HIST_ROWS, :] = prev
    gbuf[:, FFN_HIST_ROWS:FFN_HIST_ROWS + t, :] = g_ref[...]
    conv = jnp.zeros((nb, t, tk), F32) + cb_ref[...]
    for j in range(FFN_CONV_WIDTH):
        conv = conv + cw_ref[j:j + 1, :] * gbuf[:, lead + j:lead + j + t, :]
    act = (_silu(conv) * u_ref[...]).reshape(nb * t, tk).astype(BF16)

    @pl.when(kk == 0)
    def _():
        acc[...] = jnp.zeros_like(acc)

    acc[...] += _dot(act, wd_ref[...])

    @pl.when(kk == pl.num_programs(2) - 1)
    def _():
        y = acc[...].reshape(o_ref.shape)
        o_ref[...] = _layer_norm(alpha * x_ref[...] + (1.0 + gate_ref[...]) * y, lg_ref[...], lb_ref[...])


def ffn_down_norm(gu, prev, x, ada, gate_chunk, conv_w, conv_b, w_down, ln_g, ln_b, alpha, nb, t, tk,
                  prev_is_state):
    s, tt, f2 = gu.shape
    f = f2 // 2
    d = x.shape[2]
    nk = f // tk
    if prev_is_state:
        assert tt == t
        prev_spec = pl.BlockSpec((nb, prev.shape[1], tk), lambda i, r, k: (i, 0, k))
    else:
        assert nb == 1
        per = t // FFN_HIST_ROWS
        prev_spec = pl.BlockSpec((nb, FFN_HIST_ROWS, tk),
                                 lambda i, r, k: (i, jnp.maximum(r * per - 1, 0), k))
    tok = lambda i, r, k: (i, r, 0)
    return pl.pallas_call(
        functools.partial(_ffn_down_body, alpha=alpha, zero_first=not prev_is_state),
        grid=(s // nb, tt // t, nk),
        in_specs=[pl.BlockSpec((nb, t, tk), lambda i, r, k: (i, r, k)),
                  pl.BlockSpec((nb, t, tk), lambda i, r, k: (i, r, nk + k)),
                  prev_spec,
                  pl.BlockSpec((FFN_CONV_WIDTH, tk), lambda i, r, k: (0, k)),
                  pl.BlockSpec((1, tk), lambda i, r, k: (0, k)),
                  pl.BlockSpec((tk, d), lambda i, r, k: (k, 0)),
                  pl.BlockSpec((nb, t, d), tok),
                  pl.BlockSpec((nb, 1, d), lambda i, r, k: (i, 0, gate_chunk)),
                  pl.BlockSpec((1, d), lambda i, r, k: (0, 0)),
                  pl.BlockSpec((1, d), lambda i, r, k: (0, 0))],
        out_specs=pl.BlockSpec((nb, t, d), tok),
        out_shape=jax.ShapeDtypeStruct((s, tt, d), F32),
        scratch_shapes=[pltpu.VMEM((nb, FFN_HIST_ROWS + t, tk), F32),
                        pltpu.VMEM((nb * t, d), F32)],
        compiler_params=_params("arbitrary", "arbitrary", "arbitrary"),
        name="ffn_down_norm",
    )(gu, gu, prev, conv_w, conv_b.reshape(1, -1), w_down, x, ada, ln_g.reshape(1, -1), ln_b.reshape(1, -1))


def _kv_rows_body(k_ref, v_ref, ko_ref, vo_ref):
    t = k_ref.shape[1]
    for h in range(ATT_HEADS):
        cols = slice(h * ATT_DIM, (h + 1) * ATT_DIM)
        ko_ref[:, pl.ds(h, t, stride=ATT_HEADS), :] = k_ref[:, :, cols]
        vo_ref[:, pl.ds(h, t, stride=ATT_HEADS), :] = v_ref[:, :, cols]


def kv_rows(z, nb, t):
    s, tt, _ = z.shape
    out_spec = pl.BlockSpec((nb, t * ATT_HEADS, ATT_DIM), lambda i, r: (i, r, 0))
    out_shape = jax.ShapeDtypeStruct((s, tt * ATT_HEADS, ATT_DIM), F32)
    k, v = pl.pallas_call(
        _kv_rows_body,
        grid=(s // nb, tt // t),
        in_specs=[pl.BlockSpec((nb, t, ATT_W), lambda i, r: (i, r, OFF_AK // ATT_W)),
                  pl.BlockSpec((nb, t, ATT_W), lambda i, r: (i, r, OFF_AV // ATT_W))],
        out_specs=[out_spec, out_spec],
        out_shape=[out_shape, out_shape],
        compiler_params=_params("arbitrary", "arbitrary"),
        name="kv_rows",
    )(z, z)
    return k.reshape(s, tt, ATT_HEADS, ATT_DIM), v.reshape(s, tt, ATT_HEADS, ATT_DIM)


def _tile(s, t, rows):
    if t >= rows:
        assert t % rows == 0
        return 1, rows
    nb = max(1, min(s, rows // t))
    assert s % nb == 0
    return nb, t


def trunk_layer(x, ada, attend, conv_state, ret_state, ffn_state, pos0, alpha, p):
    s, t, d = x.shape
    nb, tr = _tile(s, t, 1024 if t >= 1024 else 512)
    z = modulated_matmul(x, ada, 1, 0, p["w_in"], nb, tr, 512)
    ya = attend(z)
    yb, ret_new = retention(z, p["ret_gn_g"], ret_state, pos0)
    nbc, trc = _tile(s, t, 256) if t >= 256 else _tile(s, t, 128)
    yc, conv_new = conformer_conv(z, p["conv_w"], p["conv_b"], p["conv_ln_g"], p["conv_ln_b"],
                                  conv_state, nbc, trc)
    nbm, trm = _tile(s, t, 512)
    merged = gated_merge(ya, yb, yc, z, p["w_branch"], nbm, trm, 512)
    x1 = outproj_norm(x, merged, ada, 2, p["w_o"], p["ln1_g"], p["ln1_b"], alpha, nbm, trm)
    gu = modulated_matmul(x1, ada, 4, 3, p["w_up"], nb, tr, 512)
    f = gu.shape[2] // 2
    prev = gu if ffn_state is None else ffn_state
    x2 = ffn_down_norm(gu, prev, x1, ada, 5, p["ffn_conv_w"], p["ffn_conv_b"], p["w_down"],
                       p["ln2_g"], p["ln2_b"], alpha, nbm, trm, 512, ffn_state is not None)
    k_new, v_new = kv_rows(z, nbm, trm)
    ffn_new = gu[:, t - (FFN_CONV_WIDTH - 1):, :f]
    return x2, k_new, v_new, conv_new, ret_new, ffn_new


def kernel(x_prompt, x_sample, cache_k, cache_v, state_conv, state_ret, state_ffn, page_table,
           c_prompt, c_sample, w_ada, b_ada, w_in, w_branch, w_o, ln1_g, ln1_b, ret_gn_g,
           conv_w, conv_b, conv_ln_g, conv_ln_b, w_up, ffn_conv_w, ffn_conv_b, w_down, ln2_g, ln2_b):
    depth = w_ada.shape[0]
    b = x_prompt.shape[0]
    sb = x_sample.shape[0]
    alpha = (2 * depth) ** 0.25
    past = page_table.shape[1] * PAGE_SIZE

    c_all = jnp.concatenate([c_prompt, c_sample], axis=0)
    pad = (-c_all.shape[0]) % 8
    ada = ada_all(jnp.pad(c_all, ((0, pad), (0, 0))), w_ada, b_ada)
    ada_p = ada[:, :b, None, :]
    ada_s = ada[:, b:b + sb, None, :]

    xp, xs = x_prompt, x_sample
    outs_p, outs_s = [], []
    for l in range(depth):
        p = {"w_in": w_in[l].astype(BF16), "w_branch": w_branch[l].astype(BF16), "w_o": w_o[l].astype(BF16),
             "w_up": w_up[l].astype(BF16), "w_down": w_down[l].astype(BF16),
             "ln1_g": ln1_g[l], "ln1_b": ln1_b[l], "ret_gn_g": ret_gn_g[l], "conv_w": conv_w[l],
             "conv_b": conv_b[l], "conv_ln_g": conv_ln_g[l], "conv_ln_b": conv_ln_b[l],
             "ffn_conv_w": ffn_conv_w[l], "ffn_conv_b": ffn_conv_b[l], "ln2_g": ln2_g[l], "ln2_b": ln2_b[l]}
        xp, *rest_p = trunk_layer(xp, ada_p[l], moba_prompt, None, None, None, 0, alpha, p)
        attend_s = functools.partial(moba_sample, cache_k=cache_k, cache_v=cache_v,
                                     page_table=page_table, layer=l)
        xs, *rest_s = trunk_layer(xs, ada_s[l], attend_s, state_conv[l], state_ret[l], state_ffn[l],
                                  past, alpha, p)
        outs_p.append(rest_p)
        outs_s.append(rest_s)
    stack = lambda outs, i: jnp.stack([o[i] for o in outs])
    return (xp, xs, *(stack(outs_p, i) for i in range(5)), *(stack(outs_s, i) for i in range(5)))
```

```python
import functools
import math

import jax
import jax.numpy as jnp
from jax import lax
from jax.experimental import pallas as pl
from jax.experimental.pallas import tpu as pltpu

F32 = jnp.float32
BF16 = jnp.bfloat16

PAGE_SIZE = 128
ATT_HEADS = 8
ATT_DIM = 128
ATT_W = ATT_HEADS * ATT_DIM
MOBA_BLOCK = 256
MOBA_TOPK = 3
MOBA_CHUNK_BLOCKS = 4
MOBA_HEADS_PER_STEP = 2
RET_HEADS = 4
RET_DK = 256
RET_DV = 256
RET_CHUNK = 128
CONV_CH = 1024
CONV_WIDTH = 31
FFN_CONV_WIDTH = 3
LN_EPS = 1e-5
NEG_BIG = -1e30

OFF_AQ, OFF_AK, OFF_AV = 0, ATT_W, 2 * ATT_W
OFF_BQ = 3 * ATT_W
OFF_BK = OFF_BQ + RET_HEADS * RET_DK
OFF_BV = OFF_BK + RET_HEADS * RET_DK
OFF_BG = OFF_BV + RET_HEADS * RET_DV
OFF_C = OFF_BG + RET_HEADS * RET_DV
OFF_GA = OFF_C + 2 * CONV_CH

VMEM_LIMIT_BYTES = 56 * 2**20
HIST_ROWS = 32
FFN_HIST_ROWS = 8
FFN_DOWN_CHUNK = 1408


def _params(*sem):
    return pltpu.CompilerParams(dimension_semantics=sem, vmem_limit_bytes=VMEM_LIMIT_BYTES)


def _dot(a, b):
    return jnp.dot(a, b, preferred_element_type=F32)


def _dot_nt(a, b, precision=None):
    return lax.dot_general(a, b, (((1,), (1,)), ((), ())), precision=precision,
                           preferred_element_type=F32)


def _dot_tn(a, b):
    return lax.dot_general(a, b, (((0,), (0,)), ((), ())), preferred_element_type=F32)


def _silu(x):
    return x * jax.nn.sigmoid(x)


def _layer_norm(x, g, b):
    mu = jnp.mean(x, -1, keepdims=True)
    xc = x - mu
    var = jnp.mean(xc * xc, -1, keepdims=True)
    return xc * lax.rsqrt(var + LN_EPS) * g + b


def _ada_body(c_ref, w_ref, b_ref, o_ref):
    a = _silu(c_ref[...]).astype(BF16)
    o_ref[...] = _dot(a, w_ref[...].astype(BF16)) + b_ref[...]


def ada_all(c_all, w_ada, b_ada, tn=1024):
    depth, d, n = w_ada.shape
    r = c_all.shape[0]
    return pl.pallas_call(
        _ada_body,
        grid=(depth, n // tn),
        in_specs=[pl.BlockSpec((r, d), lambda l, j: (0, 0)),
                  pl.BlockSpec((None, d, tn), lambda l, j: (l, 0, j)),
                  pl.BlockSpec((None, 1, tn), lambda l, j: (l, 0, j))],
        out_specs=pl.BlockSpec((None, r, tn), lambda l, j: (l, 0, j)),
        out_shape=jax.ShapeDtypeStruct((depth, r, n), F32),
        compiler_params=_params("arbitrary", "arbitrary"),
        name="ada",
    )(c_all, w_ada, b_ada.reshape(depth, 1, n))


def _modmm_body(x_ref, sc_ref, sh_ref, w_ref, o_ref, h_ref):
    @pl.when(pl.program_id(2) == 0)
    def _():
        h = x_ref[...] * (1.0 + sc_ref[...]) + sh_ref[...]
        h_ref[...] = h.reshape(h_ref.shape).astype(BF16)

    o_ref[...] = _dot(h_ref[...], w_ref[...]).reshape(o_ref.shape)


def modulated_matmul(x, ada, sc_chunk, sh_chunk, w, nb, t, tn):
    s, tt, d = x.shape
    n = w.shape[1]
    return pl.pallas_call(
        _modmm_body,
        grid=(s // nb, tt // t, n // tn),
        in_specs=[pl.BlockSpec((nb, t, d), lambda i, r, j: (i, r, 0)),
                  pl.BlockSpec((nb, 1, d), lambda i, r, j: (i, 0, sc_chunk)),
                  pl.BlockSpec((nb, 1, d), lambda i, r, j: (i, 0, sh_chunk)),
                  pl.BlockSpec((d, tn), lambda i, r, j: (0, j))],
        out_specs=pl.BlockSpec((nb, t, tn), lambda i, r, j: (i, r, j)),
        out_shape=jax.ShapeDtypeStruct((s, tt, n), F32),
        scratch_shapes=[pltpu.VMEM((nb * t, d), BF16)],
        compiler_params=_params("arbitrary", "arbitrary", "arbitrary"),
        name="modmm",
    )(x, ada, ada, w)


def _select_blocks(score, n_past, n_blocks, axis):
    blk = lax.broadcasted_iota(jnp.int32, score.shape, axis)
    past = blk < n_past
    s = jnp.where(past, score, -jnp.inf)
    rank = jnp.zeros(score.shape, jnp.int32)
    for m in range(n_blocks):
        sm = s[m:m + 1, :] if axis == 0 else s[:, m:m + 1]
        beats = jnp.where(sm > s, 1, jnp.where(sm == s, jnp.where(blk > m, 1, 0), 0))
        rank = rank + beats
    return jnp.where(past, jnp.where(rank < MOBA_TOPK, 1.0, 0.0), 0.0)


def _moba_prompt_body(q_ref, k_ref, v_ref, o_ref, kb_ref, vt_ref, km_ref, m_scr, l_scr, acc_scr):
    qi = pl.program_id(2)
    n_blocks = k_ref.shape[0] // MOBA_BLOCK
    scale = ATT_DIM ** -0.5
    heads = [slice(g * ATT_DIM, (g + 1) * ATT_DIM) for g in range(MOBA_HEADS_PER_STEP)]

    @pl.when(qi == 0)
    def _():
        for g, hd in enumerate(heads):
            kb_ref[g] = k_ref[:, hd].astype(BF16)
            for n in range(n_blocks):
                rows = slice(n * MOBA_BLOCK, (n + 1) * MOBA_BLOCK)
                vt_ref[g, n] = v_ref[rows, hd].T.astype(BF16)
                km_ref[g, n:n + 1, :] = jnp.mean(k_ref[rows, hd], axis=0, keepdims=True)

    start = pl.multiple_of(qi * MOBA_BLOCK, MOBA_BLOCK)
    key = lax.broadcasted_iota(jnp.int32, (MOBA_BLOCK, MOBA_BLOCK), 0)
    qry = lax.broadcasted_iota(jnp.int32, (MOBA_BLOCK, MOBA_BLOCK), 1)
    qbs, biases = [], []
    for g, hd in enumerate(heads):
        q = q_ref[:, hd]
        qb = q.astype(BF16)
        score = _dot_nt(km_ref[g], q, precision=lax.Precision.HIGHEST)
        biases.append((_select_blocks(score, qi, n_blocks, 0) - 1.0) * (-NEG_BIG))
        qbs.append(qb)
        s0 = _dot_nt(kb_ref[g, pl.ds(start, MOBA_BLOCK), :], qb) * scale
        s0 = jnp.where(key <= qry, s0, NEG_BIG)
        m0 = jnp.max(s0, axis=0, keepdims=True)
        p0 = jnp.exp(s0 - m0)
        m_scr[g] = m0
        l_scr[g] = jnp.sum(p0, axis=0, keepdims=True)
        acc_scr[g] = _dot(vt_ref[g, qi], p0.astype(BF16))

    span = MOBA_CHUNK_BLOCKS * MOBA_BLOCK
    for c in range(n_blocks // MOBA_CHUNK_BLOCKS):
        @pl.when(c * MOBA_CHUNK_BLOCKS < qi)
        def _():
            first = c * MOBA_CHUNK_BLOCKS
            for g in range(MOBA_HEADS_PER_STEP):
                s = _dot_nt(kb_ref[g, c * span:(c + 1) * span, :], qbs[g]) * scale
                s = jnp.concatenate(
                    [s[j * MOBA_BLOCK:(j + 1) * MOBA_BLOCK, :] + biases[g][first + j:first + j + 1, :]
                     for j in range(MOBA_CHUNK_BLOCKS)], axis=0)
                m = m_scr[g]
                m_new = jnp.maximum(m, jnp.max(s, axis=0, keepdims=True))
                alpha = jnp.exp(m - m_new)
                p = jnp.exp(s - m_new)
                l_scr[g] = alpha * l_scr[g] + jnp.sum(p, axis=0, keepdims=True)
                pb = p.astype(BF16)
                pv = _dot(vt_ref[g, first], pb[0:MOBA_BLOCK, :])
                for j in range(1, MOBA_CHUNK_BLOCKS):
                    pv = pv + _dot(vt_ref[g, first + j], pb[j * MOBA_BLOCK:(j + 1) * MOBA_BLOCK, :])
                acc_scr[g] = alpha * acc_scr[g] + pv
                m_scr[g] = m_new

    for g, hd in enumerate(heads):
        o_ref[:, hd] = (acc_scr[g] / l_scr[g]).T.astype(o_ref.dtype)


def moba_prompt(z):
    b, t, _ = z.shape
    assert t % (MOBA_BLOCK * MOBA_CHUNK_BLOCKS) == 0
    nq = t // MOBA_BLOCK
    assert nq % 8 == 0 and ATT_HEADS % MOBA_HEADS_PER_STEP == 0
    hps = MOBA_HEADS_PER_STEP
    w = hps * ATT_DIM
    kcol = OFF_AK // w
    vcol = OFF_AV // w
    return pl.pallas_call(
        _moba_prompt_body,
        grid=(b, ATT_HEADS // hps, nq),
        in_specs=[pl.BlockSpec((None, MOBA_BLOCK, w), lambda i, h, q: (i, q, h)),
                  pl.BlockSpec((None, t, w), lambda i, h, q: (i, 0, kcol + h)),
                  pl.BlockSpec((None, t, w), lambda i, h, q: (i, 0, vcol + h))],
        out_specs=pl.BlockSpec((None, MOBA_BLOCK, w), lambda i, h, q: (i, q, h)),
        out_shape=jax.ShapeDtypeStruct((b, t, ATT_W), BF16),
        scratch_shapes=[pltpu.VMEM((hps, t, ATT_DIM), BF16),
                        pltpu.VMEM((hps, nq, ATT_DIM, MOBA_BLOCK), BF16),
                        pltpu.VMEM((hps, nq, ATT_DIM), F32),
                        pltpu.VMEM((hps, 1, MOBA_BLOCK), F32), pltpu.VMEM((hps, 1, MOBA_BLOCK), F32),
                        pltpu.VMEM((hps, ATT_DIM, MOBA_BLOCK), F32)],
        compiler_params=_params("arbitrary", "arbitrary", "arbitrary"),
        name="moba_prompt",
    )(z, z, z)


def _softmax_partial(qb, kb, vb, bias):
    s = _dot_nt(qb, kb) * (ATT_DIM ** -0.5) + bias
    m = jnp.max(s, axis=1, keepdims=True)
    p = jnp.exp(s - m)
    return m, jnp.sum(p, axis=1, keepdims=True), _dot(p.astype(BF16), vb)


def _moba_sample_body(pt_ref, q_ref, kn_ref, vn_ref, *refs):
    del pt_ref
    n_pages = (len(refs) - 4) // 2
    k_refs, v_refs = refs[:n_pages], refs[n_pages:2 * n_pages]
    o_ref, km_ref, ko_ref, vo_ref = refs[2 * n_pages:]
    pages_per_block = MOBA_BLOCK // PAGE_SIZE
    n_blocks = n_pages // pages_per_block
    rows = q_ref.shape[0]
    hr = ATT_HEADS * rows
    pad_rows = ko_ref.shape[0]

    def head_major(x):
        return jnp.concatenate([x[:, h * ATT_DIM:(h + 1) * ATT_DIM] for h in range(ATT_HEADS)], axis=0)

    q = head_major(q_ref[...])
    qb = q.astype(BF16)

    blk_rows = MOBA_BLOCK * ATT_HEADS
    q_head = lax.broadcasted_iota(jnp.int32, (hr, blk_rows), 0) // rows
    k_head = lax.broadcasted_iota(jnp.int32, (hr, blk_rows), 1) % ATT_HEADS
    head_bias = jnp.where(q_head == k_head, 0.0, NEG_BIG)
    km_ref[...] = jnp.zeros_like(km_ref)
    parts = []
    for n in range(n_blocks):
        pages = range(n * pages_per_block, (n + 1) * pages_per_block)
        kblk = jnp.concatenate([k_refs[pg][...] for pg in pages], axis=0)
        vblk = jnp.concatenate([v_refs[pg][...] for pg in pages], axis=0)
        parts.append(_softmax_partial(qb, kblk.astype(BF16), vblk.astype(BF16), head_bias))
        km_ref[n * ATT_HEADS:(n + 1) * ATT_HEADS, :] = (
            jnp.sum(kblk.reshape(MOBA_BLOCK, ATT_HEADS, ATT_DIM), axis=0) * (1.0 / MOBA_BLOCK))

    sc = _dot_nt(q, km_ref[...], precision=lax.Precision.HIGHEST)
    col = lax.broadcasted_iota(jnp.int32, sc.shape, 1)
    row_head = lax.broadcasted_iota(jnp.int32, sc.shape, 0) // rows
    sc = jnp.where(col % ATT_HEADS == row_head, jnp.where(col < n_blocks * ATT_HEADS, sc, 0.0), 0.0)
    g_row = lax.broadcasted_iota(jnp.int32, (128, 128), 0)
    g_col = lax.broadcasted_iota(jnp.int32, (128, 128), 1)
    gather = jnp.where(g_row // ATT_HEADS == g_col, 1.0, 0.0)
    score = jnp.dot(sc, gather, precision=lax.Precision.HIGHEST, preferred_element_type=F32)
    sel = _select_blocks(score, n_blocks, n_blocks, 1)

    ko_ref[...] = jnp.zeros_like(ko_ref)
    vo_ref[...] = jnp.zeros_like(vo_ref)
    ko_ref[0:hr, :] = head_major(kn_ref[...]).astype(BF16)
    vo_ref[0:hr, :] = head_major(vn_ref[...]).astype(BF16)
    o_qrow = lax.broadcasted_iota(jnp.int32, (hr, pad_rows), 0)
    o_kcol = lax.broadcasted_iota(jnp.int32, (hr, pad_rows), 1)
    own_bias = jnp.where(o_kcol // rows == o_qrow // rows,
                         jnp.where(o_kcol % rows <= o_qrow % rows, 0.0, NEG_BIG), NEG_BIG)
    m_o, l_o, acc_o = _softmax_partial(qb, ko_ref[...], vo_ref[...], own_bias)

    m_tot = m_o
    for n in range(n_blocks):
        m_tot = jnp.maximum(m_tot, jnp.where(sel[:, n:n + 1] > 0.0, parts[n][0], NEG_BIG))
    w_o = jnp.exp(m_o - m_tot)
    num = w_o * acc_o
    den = w_o * l_o
    for n in range(n_blocks):
        m_n, l_n, acc_n = parts[n]
        w = jnp.where(sel[:, n:n + 1] > 0.0, jnp.exp(m_n - m_tot), 0.0)
        num = num + w * acc_n
        den = den + w * l_n
    out = num / den
    for h in range(ATT_HEADS):
        o_ref[:, h * ATT_DIM:(h + 1) * ATT_DIM] = out[h * rows:(h + 1) * rows, :].astype(o_ref.dtype)


def moba_sample(z, cache_k, cache_v, page_table, layer):
    s, rows, _ = z.shape
    n_pages = page_table.shape[1]
    pages_per_block = MOBA_BLOCK // PAGE_SIZE
    assert MOBA_BLOCK % PAGE_SIZE == 0 and n_pages % pages_per_block == 0
    n_blocks = n_pages // pages_per_block
    hr = ATT_HEADS * rows
    assert hr <= 128 and n_blocks * ATT_HEADS <= 128
    depth, n_phys = cache_k.shape[0], cache_k.shape[1]
    ck = cache_k.reshape(depth, n_phys, PAGE_SIZE * ATT_HEADS, ATT_DIM)
    cv = cache_v.reshape(depth, n_phys, PAGE_SIZE * ATT_HEADS, ATT_DIM)

    def page_spec(which):
        return pl.BlockSpec((None, None, PAGE_SIZE * ATT_HEADS, ATT_DIM),
                            lambda i, pt: (layer, pt[i, which], 0, 0))

    def z_spec(col):
        return pl.BlockSpec((None, rows, ATT_W), lambda i, pt: (i, 0, col))

    page_specs = [page_spec(pg) for pg in range(n_pages)]
    grid_spec = pltpu.PrefetchScalarGridSpec(
        num_scalar_prefetch=1,
        grid=(s,),
        in_specs=[z_spec(OFF_AQ // ATT_W), z_spec(OFF_AK // ATT_W), z_spec(OFF_AV // ATT_W)]
        + page_specs + page_specs,
        out_specs=pl.BlockSpec((None, rows, ATT_W), lambda i, pt: (i, 0, 0)),
        scratch_shapes=[pltpu.VMEM((128, ATT_DIM), F32),
                        pltpu.VMEM((128, ATT_DIM), BF16),
                        pltpu.VMEM((128, ATT_DIM), BF16)])
    return pl.pallas_call(
        _moba_sample_body,
        grid_spec=grid_spec,
        out_shape=jax.ShapeDtypeStruct((s, rows, ATT_W), BF16),
        compiler_params=_params("arbitrary"),
        name="moba_sample",
    )(page_table, z, z, z, *([ck] * n_pages), *([cv] * n_pages))


def _retention_body(*refs, rows, has_state):
    if has_state:
        (q_ref, k_ref, v_ref, g_ref, cos_ref, sin_ref, intra_ref, qd_ref, kd_ref, cd_ref, gn_ref,
         r0_ref, o_ref, rout_ref, r_scr) = refs
    else:
        (q_ref, k_ref, v_ref, g_ref, cos_ref, sin_ref, intra_ref, qd_ref, kd_ref, cd_ref, gn_ref,
         o_ref, rout_ref, r_scr) = refs
    ci = pl.program_id(1)
    c = intra_ref.shape[1]

    @pl.when(ci == 0)
    def _():
        if has_state:
            r_scr[...] = r0_ref[...]
        else:
            r_scr[...] = jnp.zeros_like(r_scr)

    def pad(x):
        if rows == c:
            return x
        return jnp.concatenate([x, jnp.zeros((c - rows, x.shape[1]), x.dtype)], axis=0)

    cos = pad(cos_ref[...])
    sin = pad(sin_ref[...])
    even = lax.broadcasted_iota(jnp.int32, cos.shape, 1) % 2 == 0

    def rotate(x):
        nxt = pltpu.roll(x, x.shape[1] - 1, axis=1)
        prv = pltpu.roll(x, 1, axis=1)
        return x * cos + jnp.where(even, nxt, prv) * sin

    for h in range(RET_HEADS):
        qk = slice(h * RET_DK, (h + 1) * RET_DK)
        vv = slice(h * RET_DV, (h + 1) * RET_DV)
        q = rotate(pad(q_ref[:, qk]))
        k = rotate(pad(k_ref[:, qk])) * (RET_DK ** -0.5)
        v = pad(v_ref[:, vv])
        r = r_scr[h]
        qb, kb, vb = q.astype(BF16), k.astype(BF16), v.astype(BF16)
        s = _dot_nt(qb, kb) * intra_ref[h]
        o = _dot(s.astype(BF16), vb) + _dot(qb, r.astype(BF16)) * qd_ref[h]
        r_new = r * cd_ref[h] + _dot_tn((k * kd_ref[h]).astype(BF16), vb)
        r_scr[h] = r_new

        mu = jnp.mean(o, -1, keepdims=True)
        oc = o - mu
        var = jnp.mean(oc * oc, -1, keepdims=True)
        on = oc * lax.rsqrt(var + LN_EPS) * gn_ref[:, vv]
        y = _silu(pad(g_ref[:, vv])) * on
        o_ref[:, vv] = y[0:rows, :].astype(o_ref.dtype)

    @pl.when(ci == pl.num_programs(1) - 1)
    def _():
        rout_ref[...] = r_scr[...]


def _retention_tables(t, pos0, c_true, c_pad):
    inv = 1.0 / (10000.0 ** jnp.linspace(0.0, 1.0, RET_DK // 2, dtype=F32))
    pos = pos0 + jnp.arange(t, dtype=jnp.int32)
    ang = pos.astype(F32)[:, None] * inv[None, :]
    cos = jnp.repeat(jnp.cos(ang), 2, axis=-1)
    sin = jnp.stack([-jnp.sin(ang), jnp.sin(ang)], axis=-1).reshape(t, RET_DK)
    log_g = jnp.log1p(-jnp.exp2(-5.0 - jnp.arange(RET_HEADS, dtype=F32)))
    i = jnp.arange(c_pad, dtype=F32)
    live = i < c_true
    diff = i[:, None] - i[None, :]
    intra = jnp.where(diff >= 0, jnp.exp(log_g[:, None, None] * jnp.maximum(diff, 0.0)), 0.0)
    intra = jnp.where(live[None, :, None] & live[None, None, :], intra, 0.0)
    q_decay = jnp.exp(log_g[:, None] * (i[None, :] + 1.0))
    k_decay = jnp.where(live[None, :], jnp.exp(log_g[:, None] * (c_true - 1.0 - i[None, :])), 0.0)
    chunk_decay = jnp.exp(log_g * c_true)
    return (cos, sin, intra, q_decay[:, :, None], k_decay[:, :, None], chunk_decay[:, None, None])


def retention(z, gn_g, state, layer, pos0):
    s, t, _ = z.shape
    c_true = math.gcd(RET_CHUNK, t)
    c_pad = RET_CHUNK
    n_chunks = t // c_true
    cos, sin, intra, qd, kd, cd = _retention_tables(t, pos0, c_true, c_pad)
    has_state = state is not None
    qk_w, v_w = RET_HEADS * RET_DK, RET_HEADS * RET_DV

    def zspec(off, width):
        return pl.BlockSpec((None, c_true, width), lambda i, ci: (i, ci, off // width))

    whole = lambda a: pl.BlockSpec(a.shape, lambda i, ci: (0,) * a.ndim)
    state_spec = pl.BlockSpec((None, RET_HEADS, RET_DK, RET_DV), lambda i, ci: (i, 0, 0, 0))
    gn = gn_g.reshape(1, -1)
    in_specs = [zspec(OFF_BQ, qk_w), zspec(OFF_BK, qk_w), zspec(OFF_BV, v_w), zspec(OFF_BG, v_w),
                pl.BlockSpec((c_true, RET_DK), lambda i, ci: (ci, 0)),
                pl.BlockSpec((c_true, RET_DK), lambda i, ci: (ci, 0)),
                whole(intra), whole(qd), whole(kd), whole(cd), whole(gn)]
    args = [z, z, z, z, cos, sin, intra, qd, kd, cd, gn]
    if has_state:
        in_specs.append(pl.BlockSpec((None, None, RET_HEADS, RET_DK, RET_DV),
                                     lambda i, ci: (layer, i, 0, 0, 0)))
        args.append(state)
    return pl.pallas_call(
        functools.partial(_retention_body, rows=c_true, has_state=has_state),
        grid=(s, n_chunks),
        in_specs=in_specs,
        out_specs=[pl.BlockSpec((None, c_true, v_w), lambda i, ci: (i, ci, 0)), state_spec],
        out_shape=[jax.ShapeDtypeStruct((s, t, v_w), BF16),
                   jax.ShapeDtypeStruct((s, RET_HEADS, RET_DK, RET_DV), F32)],
        scratch_shapes=[pltpu.VMEM((RET_HEADS, RET_DK, RET_DV), F32)],
        compiler_params=_params("arbitrary", "arbitrary"),
        name="retention",
    )(*args)


def _conformer_body(*refs, has_state):
    if has_state:
        a_ref, g_ref, w_ref, b_ref, lg_ref, lb_ref, st_ref, o_ref, cn_ref, buf = refs
    else:
        a_ref, g_ref, w_ref, b_ref, lg_ref, lb_ref, o_ref, cn_ref, buf = refs
    ti = pl.program_id(1)
    nb, t, ch = a_ref.shape
    keep = CONV_WIDTH - 1
    lead = HIST_ROWS - keep

    @pl.when(ti == 0)
    def _():
        buf[:, 0:HIST_ROWS, :] = jnp.zeros((nb, HIST_ROWS, ch), F32)
        if has_state:
            buf[:, lead:HIST_ROWS, :] = st_ref[...]

    buf[:, HIST_ROWS:HIST_ROWS + t, :] = a_ref[...] * jax.nn.sigmoid(g_ref[...])
    acc = jnp.zeros((nb, t, ch), F32) + b_ref[...]
    for shift in range(8):
        taps = [j for j in range(CONV_WIDTH) if (lead + j) % 8 == shift]
        span = t + (8 if shift else 0)
        part = None
        for j in taps:
            base = lead + j - shift
            term = w_ref[j:j + 1, :] * buf[:, base:base + span, :]
            part = term if part is None else part + term
        if part is not None:
            acc = acc + part[:, shift:shift + t, :]
    y = _silu(_layer_norm(acc, lg_ref[...], lb_ref[...]))
    o_ref[...] = y.astype(o_ref.dtype)

    hist = buf[:, t:t + HIST_ROWS, :]
    buf[:, 0:HIST_ROWS, :] = hist

    @pl.when(ti == pl.num_programs(1) - 1)
    def _():
        cn_ref[...] = hist[:, lead:HIST_ROWS, :]


def conformer_conv(z, conv_w, conv_b, ln_g, ln_b, state, layer, nb, t):
    s, tt, _ = z.shape
    keep = CONV_WIDTH - 1
    has_state = state is not None
    acol = OFF_C // CONV_CH
    in_specs = [pl.BlockSpec((nb, t, CONV_CH), lambda i, r: (i, r, acol)),
                pl.BlockSpec((nb, t, CONV_CH), lambda i, r: (i, r, acol + 1)),
                pl.BlockSpec((CONV_WIDTH, CONV_CH), lambda i, r: (0, 0)),
                pl.BlockSpec((1, CONV_CH), lambda i, r: (0, 0)),
                pl.BlockSpec((1, CONV_CH), lambda i, r: (0, 0)),
                pl.BlockSpec((1, CONV_CH), lambda i, r: (0, 0))]
    args = [z, z, conv_w, conv_b.reshape(1, -1), ln_g.reshape(1, -1), ln_b.reshape(1, -1)]
    if has_state:
        in_specs.append(pl.BlockSpec((None, nb, keep, CONV_CH), lambda i, r: (layer, i, 0, 0)))
        args.append(state)
    return pl.pallas_call(
        functools.partial(_conformer_body, has_state=has_state),
        grid=(s // nb, tt // t),
        in_specs=in_specs,
        out_specs=[pl.BlockSpec((nb, t, CONV_CH), lambda i, r: (i, r, 0)),
                   pl.BlockSpec((nb, keep, CONV_CH), lambda i, r: (i, 0, 0))],
        out_shape=[jax.ShapeDtypeStruct((s, tt, CONV_CH), BF16),
                   jax.ShapeDtypeStruct((s, keep, CONV_CH), F32)],
        scratch_shapes=[pltpu.VMEM((nb, HIST_ROWS + t, CONV_CH), F32)],
        compiler_params=_params("arbitrary", "arbitrary"),
        name="conformer_conv",
    )(*args)


def _merge_body(ya_ref, yb_ref, yc_ref, w_ref, ga_ref, gb_ref, gc_ref, o_ref):
    nb, t, tn = o_ref.shape

    def branch(y_ref, i, g_ref):
        y = y_ref[...].reshape(nb * t, y_ref.shape[2])
        return jax.nn.sigmoid(g_ref[...]) * _dot(y, w_ref[i]).reshape(nb, t, tn)

    o_ref[...] = (branch(ya_ref, 0, ga_ref) + branch(yb_ref, 1, gb_ref)
                  + branch(yc_ref, 2, gc_ref)).astype(o_ref.dtype)


def gated_merge(ya, yb, yc, z, w_branch, nb, t, tn):
    s, tt, bw = ya.shape
    d = w_branch.shape[2]
    gcol = OFF_GA // tn
    per = d // tn
    yspec = pl.BlockSpec((nb, t, bw), lambda i, r, j: (i, r, 0))

    def gspec(which):
        return pl.BlockSpec((nb, t, tn), lambda i, r, j: (i, r, gcol + which * per + j))

    return pl.pallas_call(
        _merge_body,
        grid=(s // nb, tt // t, d // tn),
        in_specs=[yspec, yspec, yspec,
                  pl.BlockSpec((3, bw, tn), lambda i, r, j: (0, 0, j)),
                  gspec(0), gspec(1), gspec(2)],
        out_specs=pl.BlockSpec((nb, t, tn), lambda i, r, j: (i, r, j)),
        out_shape=jax.ShapeDtypeStruct((s, tt, d), BF16),
        compiler_params=_params("arbitrary", "arbitrary", "arbitrary"),
        name="gated_merge",
    )(ya, yb, yc, w_branch, z, z, z)


def _outproj_body(x_ref, m_ref, gate_ref, w_ref, lg_ref, lb_ref, o_ref, *, alpha):
    nb, t, d = x_ref.shape
    y = _dot(m_ref[...].reshape(nb * t, m_ref.shape[2]), w_ref[...]).reshape(nb, t, d)
    o_ref[...] = _layer_norm(alpha * x_ref[...] + (1.0 + gate_ref[...]) * y, lg_ref[...], lb_ref[...])


def outproj_norm(x, merged, ada, gate_chunk, w_o, ln_g, ln_b, alpha, nb, t):
    s, tt, d = x.shape
    tok = lambda i, r: (i, r, 0)
    return pl.pallas_call(
        functools.partial(_outproj_body, alpha=alpha),
        grid=(s // nb, tt // t),
        in_specs=[pl.BlockSpec((nb, t, d), tok),
                  pl.BlockSpec((nb, t, merged.shape[2]), tok),
                  pl.BlockSpec((nb, 1, d), lambda i, r: (i, 0, gate_chunk)),
                  pl.BlockSpec(w_o.shape, lambda i, r: (0, 0)),
                  pl.BlockSpec((1, d), lambda i, r: (0, 0)),
                  pl.BlockSpec((1, d), lambda i, r: (0, 0))],
        out_specs=pl.BlockSpec((nb, t, d), tok),
        out_shape=jax.ShapeDtypeStruct((s, tt, d), F32),
        compiler_params=_params("arbitrary", "arbitrary"),
        name="outproj_norm",
    )(x, merged, ada, w_o, ln_g.reshape(1, -1), ln_b.reshape(1, -1))


def _ffn_up_body(*refs, has_state):
    if has_state:
        (x_ref, sc_ref, sh_ref, wg_ref, wu_ref, cw_ref, cb_ref, st_ref,
         act_ref, tail_ref, h_ref, gbuf, carry) = refs
    else:
        (x_ref, sc_ref, sh_ref, wg_ref, wu_ref, cw_ref, cb_ref,
         act_ref, tail_ref, h_ref, gbuf, carry) = refs
    r = pl.program_id(1)
    j = pl.program_id(2)
    nb, t, tn = act_ref.shape
    keep = FFN_CONV_WIDTH - 1
    lead = FFN_HIST_ROWS - keep

    @pl.when(j == 0)
    def _():
        h = x_ref[...] * (1.0 + sc_ref[...]) + sh_ref[...]
        h_ref[...] = h.reshape(h_ref.shape).astype(BF16)

    h = h_ref[...]
    g = _dot(h, wg_ref[...]).reshape(nb, t, tn)
    u = _dot(h, wu_ref[...]).reshape(nb, t, tn)
    if has_state:
        prev = st_ref[...]
    else:
        prev = jnp.where(r == 0, 0.0, carry[j][:, lead:FFN_HIST_ROWS, :])
    gbuf[:, lead:FFN_HIST_ROWS, :] = prev
    gbuf[:, FFN_HIST_ROWS:FFN_HIST_ROWS + t, :] = g
    conv = jnp.zeros((nb, t, tn), F32) + cb_ref[...]
    for tap in range(FFN_CONV_WIDTH):
        conv = conv + cw_ref[tap:tap + 1, :] * gbuf[:, lead + tap:lead + tap + t, :]
    act_ref[...] = (_silu(conv) * u).astype(act_ref.dtype)
    tail = g[:, t - FFN_HIST_ROWS:t, :]
    tail_ref[...] = tail
    if not has_state:
        carry[j] = tail


def ffn_up_act(x, ada, sc_chunk, sh_chunk, w_up, conv_w, conv_b, state, layer, nb, t, tn):
    s, tt, d = x.shape
    f = w_up.shape[1] // 2
    nk = f // tn
    has_state = state is not None
    assert t >= FFN_HIST_ROWS and (not has_state or tt == t) and (has_state or nb == 1)
    in_specs = [pl.BlockSpec((nb, t, d), lambda i, r, j: (i, r, 0)),
                pl.BlockSpec((nb, 1, d), lambda i, r, j: (i, 0, sc_chunk)),
                pl.BlockSpec((nb, 1, d), lambda i, r, j: (i, 0, sh_chunk)),
                pl.BlockSpec((d, tn), lambda i, r, j: (0, j)),
                pl.BlockSpec((d, tn), lambda i, r, j: (0, nk + j)),
                pl.BlockSpec((FFN_CONV_WIDTH, tn), lambda i, r, j: (0, j)),
                pl.BlockSpec((1, tn), lambda i, r, j: (0, j))]
    args = [x, ada, ada, w_up, w_up, conv_w, conv_b.reshape(1, -1)]
    if has_state:
        in_specs.append(pl.BlockSpec((None, nb, FFN_CONV_WIDTH - 1, tn), lambda i, r, j: (layer, i, 0, j)))
        args.append(state)
    return pl.pallas_call(
        functools.partial(_ffn_up_body, has_state=has_state),
        grid=(s // nb, tt // t, nk),
        in_specs=in_specs,
        out_specs=[pl.BlockSpec((nb, t, tn), lambda i, r, j: (i, r, j)),
                   pl.BlockSpec((nb, FFN_HIST_ROWS, tn), lambda i, r, j: (i, r, j))],
        out_shape=[jax.ShapeDtypeStruct((s, tt, f), BF16),
                   jax.ShapeDtypeStruct((s, (tt // t) * FFN_HIST_ROWS, f), F32)],
        scratch_shapes=[pltpu.VMEM((nb * t, d), BF16),
                        pltpu.VMEM((nb, FFN_HIST_ROWS + t, tn), F32),
                        pltpu.VMEM((nk, nb, FFN_HIST_ROWS, tn), F32)],
        compiler_params=_params("arbitrary", "arbitrary", "arbitrary"),
        name="ffn_up_act",
    )(*args)


def _ffn_down_body(a_ref, wd_ref, x_ref, gate_ref, lg_ref, lb_ref, o_ref, acc, *, alpha):
    kk = pl.program_id(2)
    nb, t, tk = a_ref.shape

    @pl.when(kk == 0)
    def _():
        acc[...] = jnp.zeros_like(acc)

    acc[...] += _dot(a_ref[...].reshape(nb * t, tk), wd_ref[...])

    @pl.when(kk == pl.num_programs(2) - 1)
    def _():
        y = acc[...].reshape(o_ref.shape)
        o_ref[...] = _layer_norm(alpha * x_ref[...] + (1.0 + gate_ref[...]) * y, lg_ref[...], lb_ref[...])


def ffn_down_norm(act, x, ada, gate_chunk, w_down, ln_g, ln_b, alpha, nb, t, tk):
    s, tt, f = act.shape
    d = x.shape[2]
    assert f % tk == 0
    tok = lambda i, r, k: (i, r, 0)
    return pl.pallas_call(
        functools.partial(_ffn_down_body, alpha=alpha),
        grid=(s // nb, tt // t, f // tk),
        in_specs=[pl.BlockSpec((nb, t, tk), lambda i, r, k: (i, r, k)),
                  pl.BlockSpec((tk, d), lambda i, r, k: (k, 0)),
                  pl.BlockSpec((nb, t, d), tok),
                  pl.BlockSpec((nb, 1, d), lambda i, r, k: (i, 0, gate_chunk)),
                  pl.BlockSpec((1, d), lambda i, r, k: (0, 0)),
                  pl.BlockSpec((1, d), lambda i, r, k: (0, 0))],
        out_specs=pl.BlockSpec((nb, t, d), tok),
        out_shape=jax.ShapeDtypeStruct((s, tt, d), F32),
        scratch_shapes=[pltpu.VMEM((nb * t, d), F32)],
        compiler_params=_params("arbitrary", "arbitrary", "arbitrary"),
        name="ffn_down_norm",
    )(act, w_down, x, ada, ln_g.reshape(1, -1), ln_b.reshape(1, -1))


def _kv_rows_body(k_ref, v_ref, ko_ref, vo_ref):
    t = k_ref.shape[1]
    for h in range(ATT_HEADS):
        cols = slice(h * ATT_DIM, (h + 1) * ATT_DIM)
        ko_ref[:, pl.ds(h, t, stride=ATT_HEADS), :] = k_ref[:, :, cols]
        vo_ref[:, pl.ds(h, t, stride=ATT_HEADS), :] = v_ref[:, :, cols]


def kv_rows(z, nb, t):
    s, tt, _ = z.shape
    out_spec = pl.BlockSpec((nb, t * ATT_HEADS, ATT_DIM), lambda i, r: (i, r, 0))
    out_shape = jax.ShapeDtypeStruct((s, tt * ATT_HEADS, ATT_DIM), F32)
    k, v = pl.pallas_call(
        _kv_rows_body,
        grid=(s // nb, tt // t),
        in_specs=[pl.BlockSpec((nb, t, ATT_W), lambda i, r: (i, r, OFF_AK // ATT_W)),
                  pl.BlockSpec((nb, t, ATT_W), lambda i, r: (i, r, OFF_AV // ATT_W))],
        out_specs=[out_spec, out_spec],
        out_shape=[out_shape, out_shape],
        compiler_params=_params("arbitrary", "arbitrary"),
        name="kv_rows",
    )(z, z)
    return k.reshape(s, tt, ATT_HEADS, ATT_DIM), v.reshape(s, tt, ATT_HEADS, ATT_DIM)


def _tile(s, t, rows):
    if t >= rows:
        assert t % rows == 0
        return 1, rows
    nb = max(1, min(s, rows // t))
    assert s % nb == 0
    return nb, t


def trunk_layer(x, ada, attend, conv_state, ret_state, ffn_state, layer, pos0, alpha, p):
    s, t, d = x.shape
    nb, tr = _tile(s, t, 1024 if t >= 1024 else 512)
    z = modulated_matmul(x, ada, 1, 0, p["w_in"], nb, tr, 512)
    ya = attend(z)
    yb, ret_new = retention(z, p["ret_gn_g"], ret_state, layer, pos0)
    nbc, trc = _tile(s, t, 256) if t >= 256 else _tile(s, t, 128)
    yc, conv_new = conformer_conv(z, p["conv_w"], p["conv_b"], p["conv_ln_g"], p["conv_ln_b"],
                                  conv_state, layer, nbc, trc)
    nbm, trm = _tile(s, t, 512)
    merged = gated_merge(ya, yb, yc, z, p["w_branch"], nbm, trm, 512)
    x1 = outproj_norm(x, merged, ada, 2, p["w_o"], p["ln1_g"], p["ln1_b"], alpha, nbm, trm)
    act, g_tail = ffn_up_act(x1, ada, 4, 3, p["w_up"], p["ffn_conv_w"], p["ffn_conv_b"], ffn_state, layer,
                             nb, tr, 512)
    x2 = ffn_down_norm(act, x1, ada, 5, p["w_down"], p["ln2_g"], p["ln2_b"], alpha, nbm, trm,
                       FFN_DOWN_CHUNK)
    k_new, v_new = kv_rows(z, nbm, trm)
    ffn_new = g_tail[:, g_tail.shape[1] - (FFN_CONV_WIDTH - 1):, :]
    return x2, k_new, v_new, conv_new, ret_new, ffn_new


def kernel(x_prompt, x_sample, cache_k, cache_v, state_conv, state_ret, state_ffn, page_table,
           c_prompt, c_sample, w_ada, b_ada, w_in, w_branch, w_o, ln1_g, ln1_b, ret_gn_g,
           conv_w, conv_b, conv_ln_g, conv_ln_b, w_up, ffn_conv_w, ffn_conv_b, w_down, ln2_g, ln2_b):
    depth = w_ada.shape[0]
    b = x_prompt.shape[0]
    sb = x_sample.shape[0]
    alpha = (2 * depth) ** 0.25
    past = page_table.shape[1] * PAGE_SIZE

    c_all = jnp.concatenate([c_prompt, c_sample], axis=0)
    pad = (-c_all.shape[0]) % 8
    ada = ada_all(jnp.pad(c_all, ((0, pad), (0, 0))), w_ada, b_ada)
    ada_p = ada[:, :b, None, :]
    ada_s = ada[:, b:b + sb, None, :]

    xp, xs = x_prompt, x_sample
    outs_p, outs_s = [], []
    for l in range(depth):
        p = {"w_in": w_in[l].astype(BF16), "w_branch": w_branch[l].astype(BF16), "w_o": w_o[l].astype(BF16),
             "w_up": w_up[l].astype(BF16), "w_down": w_down[l].astype(BF16),
             "ln1_g": ln1_g[l], "ln1_b": ln1_b[l], "ret_gn_g": ret_gn_g[l], "conv_w": conv_w[l],
             "conv_b": conv_b[l], "conv_ln_g": conv_ln_g[l], "conv_ln_b": conv_ln_b[l],
             "ffn_conv_w": ffn_conv_w[l], "ffn_conv_b": ffn_conv_b[l], "ln2_g": ln2_g[l], "ln2_b": ln2_b[l]}
        xp, *rest_p = trunk_layer(xp, ada_p[l], moba_prompt, None, None, None, l, 0, alpha, p)
        attend_s = functools.partial(moba_sample, cache_k=cache_k, cache_v=cache_v,
                                     page_table=page_table, layer=l)
        xs, *rest_s = trunk_layer(xs, ada_s[l], attend_s, state_conv, state_ret, state_ffn, l,
                                  past, alpha, p)
        outs_p.append(rest_p)
        outs_s.append(rest_s)
    stack = lambda outs, i: jnp.stack([o[i] for o in outs])
    return (xp, xs, *(stack(outs_p, i) for i in range(5)), *(stack(outs_s, i) for i in range(5)))
```

```python
import functools
import math

import jax
import jax.numpy as jnp
from jax import lax
from jax.experimental import pallas as pl
from jax.experimental.pallas import tpu as pltpu

F32 = jnp.float32
BF16 = jnp.bfloat16

PAGE_SIZE = 128
ATT_HEADS = 8
ATT_DIM = 128
ATT_W = ATT_HEADS * ATT_DIM
MOBA_BLOCK = 256
MOBA_TOPK = 3
MOBA_CHUNK_BLOCKS = 4
MOBA_HEADS_PER_STEP = 2
RET_HEADS = 4
RET_DK = 256
RET_DV = 256
RET_CHUNK = 128
CONV_CH = 1024
CONV_WIDTH = 31
FFN_CONV_WIDTH = 3
LN_EPS = 1e-5
NEG_BIG = -1e30

OFF_AQ, OFF_AK, OFF_AV = 0, ATT_W, 2 * ATT_W
OFF_BQ = 3 * ATT_W
OFF_BK = OFF_BQ + RET_HEADS * RET_DK
OFF_BV = OFF_BK + RET_HEADS * RET_DK
OFF_BG = OFF_BV + RET_HEADS * RET_DV
OFF_C = OFF_BG + RET_HEADS * RET_DV
OFF_GA = OFF_C + 2 * CONV_CH

VMEM_LIMIT_BYTES = 56 * 2**20
HIST_ROWS = 32
FFN_HIST_ROWS = 8
MERGE_GATE_COLS = 1024
FFN_DOWN_CHUNK = 1408


def _params(*sem):
    return pltpu.CompilerParams(dimension_semantics=sem, vmem_limit_bytes=VMEM_LIMIT_BYTES)


def _dot(a, b):
    return jnp.dot(a, b, preferred_element_type=F32)


def _dot_nt(a, b, precision=None):
    return lax.dot_general(a, b, (((1,), (1,)), ((), ())), precision=precision,
                           preferred_element_type=F32)


def _dot_tn(a, b):
    return lax.dot_general(a, b, (((0,), (0,)), ((), ())), preferred_element_type=F32)


def _silu(x):
    return x * jax.nn.sigmoid(x)


def _layer_norm(x, g, b):
    mu = jnp.mean(x, -1, keepdims=True)
    xc = x - mu
    var = jnp.mean(xc * xc, -1, keepdims=True)
    return xc * lax.rsqrt(var + LN_EPS) * g + b


def _ada_body(c_ref, w_ref, b_ref, o_ref):
    a = _silu(c_ref[...]).astype(BF16)
    o_ref[...] = _dot(a, w_ref[...].astype(BF16)) + b_ref[...]


def ada_all(c_all, w_ada, b_ada, tn=1024):
    depth, d, n = w_ada.shape
    r = c_all.shape[0]
    return pl.pallas_call(
        _ada_body,
        grid=(depth, n // tn),
        in_specs=[pl.BlockSpec((r, d), lambda l, j: (0, 0)),
                  pl.BlockSpec((None, d, tn), lambda l, j: (l, 0, j)),
                  pl.BlockSpec((None, 1, tn), lambda l, j: (l, 0, j))],
        out_specs=pl.BlockSpec((None, r, tn), lambda l, j: (l, 0, j)),
        out_shape=jax.ShapeDtypeStruct((depth, r, n), F32),
        compiler_params=_params("arbitrary", "arbitrary"),
        name="ada",
    )(c_all, w_ada, b_ada.reshape(depth, 1, n))


def _modmm_body(x_ref, sc_ref, sh_ref, w_ref, o_ref, h_ref):
    @pl.when(pl.program_id(2) == 0)
    def _():
        h = x_ref[...] * (1.0 + sc_ref[...]) + sh_ref[...]
        h_ref[...] = h.reshape(h_ref.shape).astype(BF16)

    o_ref[...] = _dot(h_ref[...], w_ref[...]).reshape(o_ref.shape)


def modulated_matmul(x, ada, sc_chunk, sh_chunk, w, layer, nb, t, tn):
    s, tt, d = x.shape
    n = w.shape[2]
    return pl.pallas_call(
        _modmm_body,
        grid=(s // nb, tt // t, n // tn),
        in_specs=[pl.BlockSpec((nb, t, d), lambda i, r, j: (i, r, 0)),
                  pl.BlockSpec((nb, 1, d), lambda i, r, j: (i, 0, sc_chunk)),
                  pl.BlockSpec((nb, 1, d), lambda i, r, j: (i, 0, sh_chunk)),
                  pl.BlockSpec((None, d, tn), lambda i, r, j: (layer, 0, j))],
        out_specs=pl.BlockSpec((nb, t, tn), lambda i, r, j: (i, r, j)),
        out_shape=jax.ShapeDtypeStruct((s, tt, n), F32),
        scratch_shapes=[pltpu.VMEM((nb * t, d), BF16)],
        compiler_params=_params("arbitrary", "arbitrary", "arbitrary"),
        name="modmm",
    )(x, ada, ada, w)


def _select_blocks(score, n_past, n_blocks, axis):
    blk = lax.broadcasted_iota(jnp.int32, score.shape, axis)
    past = blk < n_past
    s = jnp.where(past, score, -jnp.inf)
    rank = jnp.zeros(score.shape, jnp.int32)
    for m in range(n_blocks):
        sm = s[m:m + 1, :] if axis == 0 else s[:, m:m + 1]
        beats = jnp.where(sm > s, 1, jnp.where(sm == s, jnp.where(blk > m, 1, 0), 0))
        rank = rank + beats
    return jnp.where(past, jnp.where(rank < MOBA_TOPK, 1.0, 0.0), 0.0)


def _moba_prompt_body(q_ref, k_ref, v_ref, o_ref, kb_ref, vt_ref, km_ref, m_scr, l_scr, acc_scr):
    qi = pl.program_id(2)
    n_blocks = k_ref.shape[0] // MOBA_BLOCK
    scale = ATT_DIM ** -0.5
    heads = [slice(g * ATT_DIM, (g + 1) * ATT_DIM) for g in range(MOBA_HEADS_PER_STEP)]

    @pl.when(qi == 0)
    def _():
        for g, hd in enumerate(heads):
            kb_ref[g] = k_ref[:, hd].astype(BF16)
            for n in range(n_blocks):
                rows = slice(n * MOBA_BLOCK, (n + 1) * MOBA_BLOCK)
                vt_ref[g, n] = v_ref[rows, hd].T.astype(BF16)
                km_ref[g, n:n + 1, :] = jnp.mean(k_ref[rows, hd], axis=0, keepdims=True)

    start = pl.multiple_of(qi * MOBA_BLOCK, MOBA_BLOCK)
    key = lax.broadcasted_iota(jnp.int32, (MOBA_BLOCK, MOBA_BLOCK), 0)
    qry = lax.broadcasted_iota(jnp.int32, (MOBA_BLOCK, MOBA_BLOCK), 1)
    qbs, biases = [], []
    for g, hd in enumerate(heads):
        q = q_ref[:, hd]
        qb = q.astype(BF16)
        score = _dot_nt(km_ref[g], q, precision=lax.Precision.HIGHEST)
        biases.append((_select_blocks(score, qi, n_blocks, 0) - 1.0) * (-NEG_BIG))
        qbs.append(qb)
        s0 = _dot_nt(kb_ref[g, pl.ds(start, MOBA_BLOCK), :], qb) * scale
        s0 = jnp.where(key <= qry, s0, NEG_BIG)
        m0 = jnp.max(s0, axis=0, keepdims=True)
        p0 = jnp.exp(s0 - m0)
        m_scr[g] = m0
        l_scr[g] = jnp.sum(p0, axis=0, keepdims=True)
        acc_scr[g] = _dot(vt_ref[g, qi], p0.astype(BF16))

    span = MOBA_CHUNK_BLOCKS * MOBA_BLOCK
    for c in range(n_blocks // MOBA_CHUNK_BLOCKS):
        @pl.when(c * MOBA_CHUNK_BLOCKS < qi)
        def _():
            first = c * MOBA_CHUNK_BLOCKS
            for g in range(MOBA_HEADS_PER_STEP):
                s = _dot_nt(kb_ref[g, c * span:(c + 1) * span, :], qbs[g]) * scale
                s = jnp.concatenate(
                    [s[j * MOBA_BLOCK:(j + 1) * MOBA_BLOCK, :] + biases[g][first + j:first + j + 1, :]
                     for j in range(MOBA_CHUNK_BLOCKS)], axis=0)
                m = m_scr[g]
                m_new = jnp.maximum(m, jnp.max(s, axis=0, keepdims=True))
                alpha = jnp.exp(m - m_new)
                p = jnp.exp(s - m_new)
                l_scr[g] = alpha * l_scr[g] + jnp.sum(p, axis=0, keepdims=True)
                pb = p.astype(BF16)
                pv = _dot(vt_ref[g, first], pb[0:MOBA_BLOCK, :])
                for j in range(1, MOBA_CHUNK_BLOCKS):
                    pv = pv + _dot(vt_ref[g, first + j], pb[j * MOBA_BLOCK:(j + 1) * MOBA_BLOCK, :])
                acc_scr[g] = alpha * acc_scr[g] + pv
                m_scr[g] = m_new

    for g, hd in enumerate(heads):
        o_ref[:, hd] = (acc_scr[g] / l_scr[g]).T.astype(o_ref.dtype)


def moba_prompt(z):
    b, t, _ = z.shape
    assert t % (MOBA_BLOCK * MOBA_CHUNK_BLOCKS) == 0
    nq = t // MOBA_BLOCK
    assert nq % 8 == 0 and ATT_HEADS % MOBA_HEADS_PER_STEP == 0
    hps = MOBA_HEADS_PER_STEP
    w = hps * ATT_DIM
    kcol = OFF_AK // w
    vcol = OFF_AV // w
    return pl.pallas_call(
        _moba_prompt_body,
        grid=(b, ATT_HEADS // hps, nq),
        in_specs=[pl.BlockSpec((None, MOBA_BLOCK, w), lambda i, h, q: (i, q, h)),
                  pl.BlockSpec((None, t, w), lambda i, h, q: (i, 0, kcol + h)),
                  pl.BlockSpec((None, t, w), lambda i, h, q: (i, 0, vcol + h))],
        out_specs=pl.BlockSpec((None, MOBA_BLOCK, w), lambda i, h, q: (i, q, h)),
        out_shape=jax.ShapeDtypeStruct((b, t, ATT_W), BF16),
        scratch_shapes=[pltpu.VMEM((hps, t, ATT_DIM), BF16),
                        pltpu.VMEM((hps, nq, ATT_DIM, MOBA_BLOCK), BF16),
                        pltpu.VMEM((hps, nq, ATT_DIM), F32),
                        pltpu.VMEM((hps, 1, MOBA_BLOCK), F32), pltpu.VMEM((hps, 1, MOBA_BLOCK), F32),
                        pltpu.VMEM((hps, ATT_DIM, MOBA_BLOCK), F32)],
        compiler_params=_params("arbitrary", "arbitrary", "arbitrary"),
        name="moba_prompt",
    )(z, z, z)


def _softmax_partial(qb, kb, vb, bias):
    s = _dot_nt(qb, kb) * (ATT_DIM ** -0.5) + bias
    m = jnp.max(s, axis=1, keepdims=True)
    p = jnp.exp(s - m)
    return m, jnp.sum(p, axis=1, keepdims=True), _dot(p.astype(BF16), vb)


def _moba_sample_body(pt_ref, q_ref, kn_ref, vn_ref, *refs):
    del pt_ref
    n_pages = (len(refs) - 4) // 2
    k_refs, v_refs = refs[:n_pages], refs[n_pages:2 * n_pages]
    o_ref, km_ref, ko_ref, vo_ref = refs[2 * n_pages:]
    pages_per_block = MOBA_BLOCK // PAGE_SIZE
    n_blocks = n_pages // pages_per_block
    rows = q_ref.shape[0]
    hr = ATT_HEADS * rows
    pad_rows = ko_ref.shape[0]

    def head_major(x):
        return jnp.concatenate([x[:, h * ATT_DIM:(h + 1) * ATT_DIM] for h in range(ATT_HEADS)], axis=0)

    q = head_major(q_ref[...])
    qb = q.astype(BF16)

    blk_rows = MOBA_BLOCK * ATT_HEADS
    q_head = lax.broadcasted_iota(jnp.int32, (hr, blk_rows), 0) // rows
    k_head = lax.broadcasted_iota(jnp.int32, (hr, blk_rows), 1) % ATT_HEADS
    head_bias = jnp.where(q_head == k_head, 0.0, NEG_BIG)
    km_ref[...] = jnp.zeros_like(km_ref)
    parts = []
    for n in range(n_blocks):
        pages = range(n * pages_per_block, (n + 1) * pages_per_block)
        kblk = jnp.concatenate([k_refs[pg][...] for pg in pages], axis=0)
        vblk = jnp.concatenate([v_refs[pg][...] for pg in pages], axis=0)
        parts.append(_softmax_partial(qb, kblk.astype(BF16), vblk.astype(BF16), head_bias))
        km_ref[n * ATT_HEADS:(n + 1) * ATT_HEADS, :] = (
            jnp.sum(kblk.reshape(MOBA_BLOCK, ATT_HEADS, ATT_DIM), axis=0) * (1.0 / MOBA_BLOCK))

    sc = _dot_nt(q, km_ref[...], precision=lax.Precision.HIGHEST)
    col = lax.broadcasted_iota(jnp.int32, sc.shape, 1)
    row_head = lax.broadcasted_iota(jnp.int32, sc.shape, 0) // rows
    sc = jnp.where(col % ATT_HEADS == row_head, jnp.where(col < n_blocks * ATT_HEADS, sc, 0.0), 0.0)
    g_row = lax.broadcasted_iota(jnp.int32, (128, 128), 0)
    g_col = lax.broadcasted_iota(jnp.int32, (128, 128), 1)
    gather = jnp.where(g_row // ATT_HEADS == g_col, 1.0, 0.0)
    score = jnp.dot(sc, gather, precision=lax.Precision.HIGHEST, preferred_element_type=F32)
    sel = _select_blocks(score, n_blocks, n_blocks, 1)

    ko_ref[...] = jnp.zeros_like(ko_ref)
    vo_ref[...] = jnp.zeros_like(vo_ref)
    ko_ref[0:hr, :] = head_major(kn_ref[...]).astype(BF16)
    vo_ref[0:hr, :] = head_major(vn_ref[...]).astype(BF16)
    o_qrow = lax.broadcasted_iota(jnp.int32, (hr, pad_rows), 0)
    o_kcol = lax.broadcasted_iota(jnp.int32, (hr, pad_rows), 1)
    own_bias = jnp.where(o_kcol // rows == o_qrow // rows,
                         jnp.where(o_kcol % rows <= o_qrow % rows, 0.0, NEG_BIG), NEG_BIG)
    m_o, l_o, acc_o = _softmax_partial(qb, ko_ref[...], vo_ref[...], own_bias)

    m_tot = m_o
    for n in range(n_blocks):
        m_tot = jnp.maximum(m_tot, jnp.where(sel[:, n:n + 1] > 0.0, parts[n][0], NEG_BIG))
    w_o = jnp.exp(m_o - m_tot)
    num = w_o * acc_o
    den = w_o * l_o
    for n in range(n_blocks):
        m_n, l_n, acc_n = parts[n]
        w = jnp.where(sel[:, n:n + 1] > 0.0, jnp.exp(m_n - m_tot), 0.0)
        num = num + w * acc_n
        den = den + w * l_n
    out = num / den
    for h in range(ATT_HEADS):
        o_ref[:, h * ATT_DIM:(h + 1) * ATT_DIM] = out[h * rows:(h + 1) * rows, :].astype(o_ref.dtype)


def moba_sample(z, cache_k, cache_v, page_table, layer):
    s, rows, _ = z.shape
    n_pages = page_table.shape[1]
    pages_per_block = MOBA_BLOCK // PAGE_SIZE
    assert MOBA_BLOCK % PAGE_SIZE == 0 and n_pages % pages_per_block == 0
    n_blocks = n_pages // pages_per_block
    hr = ATT_HEADS * rows
    assert hr <= 128 and n_blocks * ATT_HEADS <= 128
    depth, n_phys = cache_k.shape[0], cache_k.shape[1]
    ck = cache_k.reshape(depth, n_phys, PAGE_SIZE * ATT_HEADS, ATT_DIM)
    cv = cache_v.reshape(depth, n_phys, PAGE_SIZE * ATT_HEADS, ATT_DIM)

    def page_spec(which):
        return pl.BlockSpec((None, None, PAGE_SIZE * ATT_HEADS, ATT_DIM),
                            lambda i, pt: (layer, pt[i, which], 0, 0))

    def z_spec(col):
        return pl.BlockSpec((None, rows, ATT_W), lambda i, pt: (i, 0, col))

    page_specs = [page_spec(pg) for pg in range(n_pages)]
    grid_spec = pltpu.PrefetchScalarGridSpec(
        num_scalar_prefetch=1,
        grid=(s,),
        in_specs=[z_spec(OFF_AQ // ATT_W), z_spec(OFF_AK // ATT_W), z_spec(OFF_AV // ATT_W)]
        + page_specs + page_specs,
        out_specs=pl.BlockSpec((None, rows, ATT_W), lambda i, pt: (i, 0, 0)),
        scratch_shapes=[pltpu.VMEM((128, ATT_DIM), F32),
                        pltpu.VMEM((128, ATT_DIM), BF16),
                        pltpu.VMEM((128, ATT_DIM), BF16)])
    return pl.pallas_call(
        _moba_sample_body,
        grid_spec=grid_spec,
        out_shape=jax.ShapeDtypeStruct((s, rows, ATT_W), BF16),
        compiler_params=_params("arbitrary"),
        name="moba_sample",
    )(page_table, z, z, z, *([ck] * n_pages), *([cv] * n_pages))


def _retention_body(*refs, rows, has_state):
    if has_state:
        (q_ref, k_ref, v_ref, g_ref, cos_ref, sin_ref, intra_ref, qd_ref, kd_ref, cd_ref, gn_ref,
         r0_ref, o_ref, rout_ref, r_scr) = refs
    else:
        (q_ref, k_ref, v_ref, g_ref, cos_ref, sin_ref, intra_ref, qd_ref, kd_ref, cd_ref, gn_ref,
         o_ref, rout_ref, r_scr) = refs
    ci = pl.program_id(1)
    c = intra_ref.shape[1]

    @pl.when(ci == 0)
    def _():
        if has_state:
            r_scr[...] = r0_ref[...]
        else:
            r_scr[...] = jnp.zeros_like(r_scr)

    def pad(x):
        if rows == c:
            return x
        return jnp.concatenate([x, jnp.zeros((c - rows, x.shape[1]), x.dtype)], axis=0)

    cos = pad(cos_ref[...])
    sin = pad(sin_ref[...])
    even = lax.broadcasted_iota(jnp.int32, cos.shape, 1) % 2 == 0

    def rotate(x):
        nxt = pltpu.roll(x, x.shape[1] - 1, axis=1)
        prv = pltpu.roll(x, 1, axis=1)
        return x * cos + jnp.where(even, nxt, prv) * sin

    for h in range(RET_HEADS):
        qk = slice(h * RET_DK, (h + 1) * RET_DK)
        vv = slice(h * RET_DV, (h + 1) * RET_DV)
        q = rotate(pad(q_ref[:, qk]))
        k = rotate(pad(k_ref[:, qk])) * (RET_DK ** -0.5)
        v = pad(v_ref[:, vv])
        r = r_scr[h]
        qb, kb, vb = q.astype(BF16), k.astype(BF16), v.astype(BF16)
        s = _dot_nt(qb, kb) * intra_ref[h]
        o = _dot(s.astype(BF16), vb) + _dot(qb, r.astype(BF16)) * qd_ref[h]
        r_new = r * cd_ref[h] + _dot_tn((k * kd_ref[h]).astype(BF16), vb)
        r_scr[h] = r_new

        mu = jnp.mean(o, -1, keepdims=True)
        oc = o - mu
        var = jnp.mean(oc * oc, -1, keepdims=True)
        on = oc * lax.rsqrt(var + LN_EPS) * gn_ref[:, vv]
        y = _silu(pad(g_ref[:, vv])) * on
        o_ref[:, vv] = y[0:rows, :].astype(o_ref.dtype)

    @pl.when(ci == pl.num_programs(1) - 1)
    def _():
        rout_ref[...] = r_scr[...]


def _retention_tables(t, pos0, c_true, c_pad):
    inv = 1.0 / (10000.0 ** jnp.linspace(0.0, 1.0, RET_DK // 2, dtype=F32))
    pos = pos0 + jnp.arange(t, dtype=jnp.int32)
    ang = pos.astype(F32)[:, None] * inv[None, :]
    cos = jnp.repeat(jnp.cos(ang), 2, axis=-1)
    sin = jnp.stack([-jnp.sin(ang), jnp.sin(ang)], axis=-1).reshape(t, RET_DK)
    log_g = jnp.log1p(-jnp.exp2(-5.0 - jnp.arange(RET_HEADS, dtype=F32)))
    i = jnp.arange(c_pad, dtype=F32)
    live = i < c_true
    diff = i[:, None] - i[None, :]
    intra = jnp.where(diff >= 0, jnp.exp(log_g[:, None, None] * jnp.maximum(diff, 0.0)), 0.0)
    intra = jnp.where(live[None, :, None] & live[None, None, :], intra, 0.0)
    q_decay = jnp.exp(log_g[:, None] * (i[None, :] + 1.0))
    k_decay = jnp.where(live[None, :], jnp.exp(log_g[:, None] * (c_true - 1.0 - i[None, :])), 0.0)
    chunk_decay = jnp.exp(log_g * c_true)
    return (cos, sin, intra, q_decay[:, :, None], k_decay[:, :, None], chunk_decay[:, None, None])


def retention(z, gn_g, state, layer, pos0):
    s, t, _ = z.shape
    c_true = math.gcd(RET_CHUNK, t)
    c_pad = RET_CHUNK
    n_chunks = t // c_true
    cos, sin, intra, qd, kd, cd = _retention_tables(t, pos0, c_true, c_pad)
    has_state = state is not None
    qk_w, v_w = RET_HEADS * RET_DK, RET_HEADS * RET_DV

    def zspec(off, width):
        return pl.BlockSpec((None, c_true, width), lambda i, ci: (i, ci, off // width))

    whole = lambda a: pl.BlockSpec(a.shape, lambda i, ci: (0,) * a.ndim)
    state_spec = pl.BlockSpec((None, RET_HEADS, RET_DK, RET_DV), lambda i, ci: (i, 0, 0, 0))
    gn = gn_g.reshape(1, -1)
    in_specs = [zspec(OFF_BQ, qk_w), zspec(OFF_BK, qk_w), zspec(OFF_BV, v_w), zspec(OFF_BG, v_w),
                pl.BlockSpec((c_true, RET_DK), lambda i, ci: (ci, 0)),
                pl.BlockSpec((c_true, RET_DK), lambda i, ci: (ci, 0)),
                whole(intra), whole(qd), whole(kd), whole(cd), whole(gn)]
    args = [z, z, z, z, cos, sin, intra, qd, kd, cd, gn]
    if has_state:
        in_specs.append(pl.BlockSpec((None, None, RET_HEADS, RET_DK, RET_DV),
                                     lambda i, ci: (layer, i, 0, 0, 0)))
        args.append(state)
    return pl.pallas_call(
        functools.partial(_retention_body, rows=c_true, has_state=has_state),
        grid=(s, n_chunks),
        in_specs=in_specs,
        out_specs=[pl.BlockSpec((None, c_true, v_w), lambda i, ci: (i, ci, 0)), state_spec],
        out_shape=[jax.ShapeDtypeStruct((s, t, v_w), BF16),
                   jax.ShapeDtypeStruct((s, RET_HEADS, RET_DK, RET_DV), F32)],
        scratch_shapes=[pltpu.VMEM((RET_HEADS, RET_DK, RET_DV), F32)],
        compiler_params=_params("arbitrary", "arbitrary"),
        name="retention",
    )(*args)


def _conformer_body(*refs, has_state):
    if has_state:
        a_ref, g_ref, w_ref, b_ref, lg_ref, lb_ref, st_ref, o_ref, cn_ref, buf = refs
    else:
        a_ref, g_ref, w_ref, b_ref, lg_ref, lb_ref, o_ref, cn_ref, buf = refs
    ti = pl.program_id(1)
    nb, t, ch = a_ref.shape
    keep = CONV_WIDTH - 1
    lead = HIST_ROWS - keep

    @pl.when(ti == 0)
    def _():
        buf[:, 0:HIST_ROWS, :] = jnp.zeros((nb, HIST_ROWS, ch), F32)
        if has_state:
            buf[:, lead:HIST_ROWS, :] = st_ref[...]

    buf[:, HIST_ROWS:HIST_ROWS + t, :] = a_ref[...] * jax.nn.sigmoid(g_ref[...])
    acc = jnp.zeros((nb, t, ch), F32) + b_ref[...]
    for shift in range(8):
        taps = [j for j in range(CONV_WIDTH) if (lead + j) % 8 == shift]
        span = t + (8 if shift else 0)
        part = None
        for j in taps:
            base = lead + j - shift
            term = w_ref[j:j + 1, :] * buf[:, base:base + span, :]
            part = term if part is None else part + term
        if part is not None:
            acc = acc + part[:, shift:shift + t, :]
    y = _silu(_layer_norm(acc, lg_ref[...], lb_ref[...]))
    o_ref[...] = y.astype(o_ref.dtype)

    hist = buf[:, t:t + HIST_ROWS, :]
    buf[:, 0:HIST_ROWS, :] = hist

    @pl.when(ti == pl.num_programs(1) - 1)
    def _():
        cn_ref[...] = hist[:, lead:HIST_ROWS, :]


def conformer_conv(z, conv_w, conv_b, ln_g, ln_b, state, layer, nb, t):
    s, tt, _ = z.shape
    keep = CONV_WIDTH - 1
    has_state = state is not None
    acol = OFF_C // CONV_CH
    in_specs = [pl.BlockSpec((nb, t, CONV_CH), lambda i, r: (i, r, acol)),
                pl.BlockSpec((nb, t, CONV_CH), lambda i, r: (i, r, acol + 1)),
                pl.BlockSpec((CONV_WIDTH, CONV_CH), lambda i, r: (0, 0)),
                pl.BlockSpec((1, CONV_CH), lambda i, r: (0, 0)),
                pl.BlockSpec((1, CONV_CH), lambda i, r: (0, 0)),
                pl.BlockSpec((1, CONV_CH), lambda i, r: (0, 0))]
    args = [z, z, conv_w, conv_b.reshape(1, -1), ln_g.reshape(1, -1), ln_b.reshape(1, -1)]
    if has_state:
        in_specs.append(pl.BlockSpec((None, nb, keep, CONV_CH), lambda i, r: (layer, i, 0, 0)))
        args.append(state)
    return pl.pallas_call(
        functools.partial(_conformer_body, has_state=has_state),
        grid=(s // nb, tt // t),
        in_specs=in_specs,
        out_specs=[pl.BlockSpec((nb, t, CONV_CH), lambda i, r: (i, r, 0)),
                   pl.BlockSpec((nb, keep, CONV_CH), lambda i, r: (i, 0, 0))],
        out_shape=[jax.ShapeDtypeStruct((s, tt, CONV_CH), BF16),
                   jax.ShapeDtypeStruct((s, keep, CONV_CH), F32)],
        scratch_shapes=[pltpu.VMEM((nb, HIST_ROWS + t, CONV_CH), F32)],
        compiler_params=_params("arbitrary", "arbitrary"),
        name="conformer_conv",
    )(*args)


def _merge_body(ya_ref, yb_ref, yc_ref, w_ref, *refs):
    o_ref = refs[-1]
    g_refs = refs[:-1]
    nb, t, d = o_ref.shape
    parts = len(g_refs) // 3
    ys = [y_ref[...].reshape(nb * t, y_ref.shape[2]) for y_ref in (ya_ref, yb_ref, yc_ref)]
    for part in range(parts):
        cols = slice(part * MERGE_GATE_COLS, (part + 1) * MERGE_GATE_COLS)
        total = None
        for i in range(3):
            term = (jax.nn.sigmoid(g_refs[i * parts + part][...])
                    * _dot(ys[i], w_ref[i, :, cols]).reshape(nb, t, MERGE_GATE_COLS))
            total = term if total is None else total + term
        o_ref[:, :, cols] = total.astype(o_ref.dtype)


def gated_merge(ya, yb, yc, z, w_branch, layer, nb, t):
    s, tt, bw = ya.shape
    d = w_branch.shape[3]
    assert d % MERGE_GATE_COLS == 0 and OFF_GA % MERGE_GATE_COLS == 0
    parts = d // MERGE_GATE_COLS
    gcol = OFF_GA // MERGE_GATE_COLS
    yspec = pl.BlockSpec((nb, t, bw), lambda i, r: (i, r, 0))

    def gspec(col):
        return pl.BlockSpec((nb, t, MERGE_GATE_COLS), lambda i, r: (i, r, gcol + col))

    return pl.pallas_call(
        _merge_body,
        grid=(s // nb, tt // t),
        in_specs=[yspec, yspec, yspec,
                  pl.BlockSpec((None, 3, bw, d), lambda i, r: (layer, 0, 0, 0))]
        + [gspec(col) for col in range(3 * parts)],
        out_specs=pl.BlockSpec((nb, t, d), lambda i, r: (i, r, 0)),
        out_shape=jax.ShapeDtypeStruct((s, tt, d), BF16),
        compiler_params=_params("arbitrary", "arbitrary"),
        name="gated_merge",
    )(ya, yb, yc, w_branch, *([z] * (3 * parts)))


def _outproj_body(x_ref, m_ref, gate_ref, w_ref, lg_ref, lb_ref, o_ref, *, alpha):
    nb, t, d = x_ref.shape
    y = _dot(m_ref[...].reshape(nb * t, m_ref.shape[2]), w_ref[...]).reshape(nb, t, d)
    o_ref[...] = _layer_norm(alpha * x_ref[...] + (1.0 + gate_ref[...]) * y, lg_ref[...], lb_ref[...])


def outproj_norm(x, merged, ada, gate_chunk, w_o, layer, ln_g, ln_b, alpha, nb, t):
    s, tt, d = x.shape
    tok = lambda i, r: (i, r, 0)
    return pl.pallas_call(
        functools.partial(_outproj_body, alpha=alpha),
        grid=(s // nb, tt // t),
        in_specs=[pl.BlockSpec((nb, t, d), tok),
                  pl.BlockSpec((nb, t, merged.shape[2]), tok),
                  pl.BlockSpec((nb, 1, d), lambda i, r: (i, 0, gate_chunk)),
                  pl.BlockSpec((None,) + w_o.shape[1:], lambda i, r: (layer, 0, 0)),
                  pl.BlockSpec((1, d), lambda i, r: (0, 0)),
                  pl.BlockSpec((1, d), lambda i, r: (0, 0))],
        out_specs=pl.BlockSpec((nb, t, d), tok),
        out_shape=jax.ShapeDtypeStruct((s, tt, d), F32),
        compiler_params=_params("arbitrary", "arbitrary"),
        name="outproj_norm",
    )(x, merged, ada, w_o, ln_g.reshape(1, -1), ln_b.reshape(1, -1))


def _ffn_up_body(*refs, has_state):
    if has_state:
        (x_ref, sc_ref, sh_ref, wg_ref, wu_ref, cw_ref, cb_ref, st_ref,
         act_ref, tail_ref, h_ref, gbuf, carry) = refs
    else:
        (x_ref, sc_ref, sh_ref, wg_ref, wu_ref, cw_ref, cb_ref,
         act_ref, tail_ref, h_ref, gbuf, carry) = refs
    r = pl.program_id(1)
    j = pl.program_id(2)
    nb, t, tn = act_ref.shape
    keep = FFN_CONV_WIDTH - 1
    lead = FFN_HIST_ROWS - keep

    @pl.when(j == 0)
    def _():
        h = x_ref[...] * (1.0 + sc_ref[...]) + sh_ref[...]
        h_ref[...] = h.reshape(h_ref.shape).astype(BF16)

    h = h_ref[...]
    g = _dot(h, wg_ref[...]).reshape(nb, t, tn)
    u = _dot(h, wu_ref[...]).reshape(nb, t, tn)
    if has_state:
        prev = st_ref[...]
    else:
        prev = jnp.where(r == 0, 0.0, carry[j][:, lead:FFN_HIST_ROWS, :])
    gbuf[:, lead:FFN_HIST_ROWS, :] = prev
    gbuf[:, FFN_HIST_ROWS:FFN_HIST_ROWS + t, :] = g
    conv = jnp.zeros((nb, t, tn), F32) + cb_ref[...]
    for tap in range(FFN_CONV_WIDTH):
        conv = conv + cw_ref[tap:tap + 1, :] * gbuf[:, lead + tap:lead + tap + t, :]
    act_ref[...] = (_silu(conv) * u).astype(act_ref.dtype)
    tail = g[:, t - FFN_HIST_ROWS:t, :]
    tail_ref[...] = tail
    if not has_state:
        carry[j] = tail


def ffn_up_act(x, ada, sc_chunk, sh_chunk, w_up, conv_w, conv_b, state, layer, nb, t, tn):
    s, tt, d = x.shape
    f = w_up.shape[2] // 2
    nk = f // tn
    has_state = state is not None
    assert t >= FFN_HIST_ROWS and (not has_state or tt == t) and (has_state or nb == 1)
    in_specs = [pl.BlockSpec((nb, t, d), lambda i, r, j: (i, r, 0)),
                pl.BlockSpec((nb, 1, d), lambda i, r, j: (i, 0, sc_chunk)),
                pl.BlockSpec((nb, 1, d), lambda i, r, j: (i, 0, sh_chunk)),
                pl.BlockSpec((None, d, tn), lambda i, r, j: (layer, 0, j)),
                pl.BlockSpec((None, d, tn), lambda i, r, j: (layer, 0, nk + j)),
                pl.BlockSpec((FFN_CONV_WIDTH, tn), lambda i, r, j: (0, j)),
                pl.BlockSpec((1, tn), lambda i, r, j: (0, j))]
    args = [x, ada, ada, w_up, w_up, conv_w, conv_b.reshape(1, -1)]
    if has_state:
        in_specs.append(pl.BlockSpec((None, nb, FFN_CONV_WIDTH - 1, tn), lambda i, r, j: (layer, i, 0, j)))
        args.append(state)
    return pl.pallas_call(
        functools.partial(_ffn_up_body, has_state=has_state),
        grid=(s // nb, tt // t, nk),
        in_specs=in_specs,
        out_specs=[pl.BlockSpec((nb, t, tn), lambda i, r, j: (i, r, j)),
                   pl.BlockSpec((nb, FFN_HIST_ROWS, tn), lambda i, r, j: (i, r, j))],
        out_shape=[jax.ShapeDtypeStruct((s, tt, f), BF16),
                   jax.ShapeDtypeStruct((s, (tt // t) * FFN_HIST_ROWS, f), F32)],
        scratch_shapes=[pltpu.VMEM((nb * t, d), BF16),
                        pltpu.VMEM((nb, FFN_HIST_ROWS + t, tn), F32),
                        pltpu.VMEM((nk, nb, FFN_HIST_ROWS, tn), F32)],
        compiler_params=_params("arbitrary", "arbitrary", "arbitrary"),
        name="ffn_up_act",
    )(*args)


def _ffn_down_body(a_ref, wd_ref, x_ref, gate_ref, lg_ref, lb_ref, o_ref, acc, *, alpha):
    kk = pl.program_id(2)
    nb, t, tk = a_ref.shape

    @pl.when(kk == 0)
    def _():
        acc[...] = jnp.zeros_like(acc)

    acc[...] += _dot(a_ref[...].reshape(nb * t, tk), wd_ref[...])

    @pl.when(kk == pl.num_programs(2) - 1)
    def _():
        y = acc[...].reshape(o_ref.shape)
        o_ref[...] = _layer_norm(alpha * x_ref[...] + (1.0 + gate_ref[...]) * y, lg_ref[...], lb_ref[...])


def ffn_down_norm(act, x, ada, gate_chunk, w_down, layer, ln_g, ln_b, alpha, nb, t, tk):
    s, tt, f = act.shape
    d = x.shape[2]
    assert f % tk == 0
    tok = lambda i, r, k: (i, r, 0)
    return pl.pallas_call(
        functools.partial(_ffn_down_body, alpha=alpha),
        grid=(s // nb, tt // t, f // tk),
        in_specs=[pl.BlockSpec((nb, t, tk), lambda i, r, k: (i, r, k)),
                  pl.BlockSpec((None, tk, d), lambda i, r, k: (layer, k, 0)),
                  pl.BlockSpec((nb, t, d), tok),
                  pl.BlockSpec((nb, 1, d), lambda i, r, k: (i, 0, gate_chunk)),
                  pl.BlockSpec((1, d), lambda i, r, k: (0, 0)),
                  pl.BlockSpec((1, d), lambda i, r, k: (0, 0))],
        out_specs=pl.BlockSpec((nb, t, d), tok),
        out_shape=jax.ShapeDtypeStruct((s, tt, d), F32),
        scratch_shapes=[pltpu.VMEM((nb * t, d), F32)],
        compiler_params=_params("arbitrary", "arbitrary", "arbitrary"),
        name="ffn_down_norm",
    )(act, w_down, x, ada, ln_g.reshape(1, -1), ln_b.reshape(1, -1))


def _kv_rows_body(k_ref, v_ref, ko_ref, vo_ref):
    t = k_ref.shape[1]
    for h in range(ATT_HEADS):
        cols = slice(h * ATT_DIM, (h + 1) * ATT_DIM)
        ko_ref[:, pl.ds(h, t, stride=ATT_HEADS), :] = k_ref[:, :, cols]
        vo_ref[:, pl.ds(h, t, stride=ATT_HEADS), :] = v_ref[:, :, cols]


def kv_rows(z, nb, t):
    s, tt, _ = z.shape
    out_spec = pl.BlockSpec((nb, t * ATT_HEADS, ATT_DIM), lambda i, r: (i, r, 0))
    out_shape = jax.ShapeDtypeStruct((s, tt * ATT_HEADS, ATT_DIM), F32)
    k, v = pl.pallas_call(
        _kv_rows_body,
        grid=(s // nb, tt // t),
        in_specs=[pl.BlockSpec((nb, t, ATT_W), lambda i, r: (i, r, OFF_AK // ATT_W)),
                  pl.BlockSpec((nb, t, ATT_W), lambda i, r: (i, r, OFF_AV // ATT_W))],
        out_specs=[out_spec, out_spec],
        out_shape=[out_shape, out_shape],
        compiler_params=_params("arbitrary", "arbitrary"),
        name="kv_rows",
    )(z, z)
    return k.reshape(s, tt, ATT_HEADS, ATT_DIM), v.reshape(s, tt, ATT_HEADS, ATT_DIM)


def _tile(s, t, rows):
    if t >= rows:
        assert t % rows == 0
        return 1, rows
    nb = max(1, min(s, rows // t))
    assert s % nb == 0
    return nb, t


def trunk_layer(x, ada, attend, conv_state, ret_state, ffn_state, layer, pos0, alpha, p):
    s, t, d = x.shape
    nb, tr = _tile(s, t, 1024 if t >= 1024 else 512)
    z = modulated_matmul(x, ada, 1, 0, p["w_in"], layer, nb, tr, 512)
    ya = attend(z)
    yb, ret_new = retention(z, p["ret_gn_g"], ret_state, layer, pos0)
    nbc, trc = _tile(s, t, 256) if t >= 256 else _tile(s, t, 128)
    yc, conv_new = conformer_conv(z, p["conv_w"], p["conv_b"], p["conv_ln_g"], p["conv_ln_b"],
                                  conv_state, layer, nbc, trc)
    nbm, trm = _tile(s, t, 512)
    nbg, trg = _tile(s, t, 256)
    merged = gated_merge(ya, yb, yc, z, p["w_branch"], layer, nbg, trg)
    x1 = outproj_norm(x, merged, ada, 2, p["w_o"], layer, p["ln1_g"], p["ln1_b"], alpha, nbm, trm)
    act, g_tail = ffn_up_act(x1, ada, 4, 3, p["w_up"], p["ffn_conv_w"], p["ffn_conv_b"], ffn_state, layer,
                             nb, tr, 512)
    x2 = ffn_down_norm(act, x1, ada, 5, p["w_down"], layer, p["ln2_g"], p["ln2_b"], alpha, nbm, trm,
                       FFN_DOWN_CHUNK)
    k_new, v_new = kv_rows(z, nbm, trm)
    ffn_new = g_tail[:, g_tail.shape[1] - (FFN_CONV_WIDTH - 1):, :]
    return x2, k_new, v_new, conv_new, ret_new, ffn_new


def kernel(x_prompt, x_sample, cache_k, cache_v, state_conv, state_ret, state_ffn, page_table,
           c_prompt, c_sample, w_ada, b_ada, w_in, w_branch, w_o, ln1_g, ln1_b, ret_gn_g,
           conv_w, conv_b, conv_ln_g, conv_ln_b, w_up, ffn_conv_w, ffn_conv_b, w_down, ln2_g, ln2_b):
    depth = w_ada.shape[0]
    b = x_prompt.shape[0]
    sb = x_sample.shape[0]
    alpha = (2 * depth) ** 0.25
    past = page_table.shape[1] * PAGE_SIZE

    c_all = jnp.concatenate([c_prompt, c_sample], axis=0)
    pad = (-c_all.shape[0]) % 8
    ada = ada_all(jnp.pad(c_all, ((0, pad), (0, 0))), w_ada, b_ada)
    ada_p = ada[:, :b, None, :]
    ada_s = ada[:, b:b + sb, None, :]

    wb = {"w_in": w_in.astype(BF16), "w_branch": w_branch.astype(BF16), "w_o": w_o.astype(BF16),
          "w_up": w_up.astype(BF16), "w_down": w_down.astype(BF16)}
    xp, xs = x_prompt, x_sample
    outs_p, outs_s = [], []
    for l in range(depth):
        p = {**wb,
             "ln1_g": ln1_g[l], "ln1_b": ln1_b[l], "ret_gn_g": ret_gn_g[l], "conv_w": conv_w[l],
             "conv_b": conv_b[l], "conv_ln_g": conv_ln_g[l], "conv_ln_b": conv_ln_b[l],
             "ffn_conv_w": ffn_conv_w[l], "ffn_conv_b": ffn_conv_b[l], "ln2_g": ln2_g[l], "ln2_b": ln2_b[l]}
        xp, *rest_p = trunk_layer(xp, ada_p[l], moba_prompt, None, None, None, l, 0, alpha, p)
        attend_s = functools.partial(moba_sample, cache_k=cache_k, cache_v=cache_v,
                                     page_table=page_table, layer=l)
        xs, *rest_s = trunk_layer(xs, ada_s[l], attend_s, state_conv, state_ret, state_ffn, l,
                                  past, alpha, p)
        outs_p.append(rest_p)
        outs_s.append(rest_s)
    stack = lambda outs, i: jnp.stack([o[i] for o in outs])
    return (xp, xs, *(stack(outs_p, i) for i in range(5)), *(stack(outs_s, i) for i in range(5)))
```

```python
import functools
import math

import jax
import jax.numpy as jnp
from jax import lax
from jax.experimental import pallas as pl
from jax.experimental.pallas import tpu as pltpu

F32 = jnp.float32
BF16 = jnp.bfloat16

PAGE_SIZE = 128
ATT_HEADS = 8
ATT_DIM = 128
ATT_W = ATT_HEADS * ATT_DIM
MOBA_BLOCK = 256
MOBA_TOPK = 3
MOBA_CHUNK_BLOCKS = 4
MOBA_HEADS_PER_STEP = 2
RET_HEADS = 4
RET_DK = 256
RET_DV = 256
RET_CHUNK = 128
CONV_CH = 1024
CONV_WIDTH = 31
FFN_CONV_WIDTH = 3
LN_EPS = 1e-5
NEG_BIG = -1e30

OFF_AQ, OFF_AK, OFF_AV = 0, ATT_W, 2 * ATT_W
OFF_BQ = 3 * ATT_W
OFF_BK = OFF_BQ + RET_HEADS * RET_DK
OFF_BV = OFF_BK + RET_HEADS * RET_DK
OFF_BG = OFF_BV + RET_HEADS * RET_DV
OFF_C = OFF_BG + RET_HEADS * RET_DV
OFF_GA = OFF_C + 2 * CONV_CH

VMEM_LIMIT_BYTES = 56 * 2**20
HIST_ROWS = 32
FFN_HIST_ROWS = 8
MERGE_GATE_COLS = 1024
FFN_DOWN_CHUNK = 1408


def _params(*sem):
    return pltpu.CompilerParams(dimension_semantics=sem, vmem_limit_bytes=VMEM_LIMIT_BYTES)


def _dot(a, b):
    return jnp.dot(a, b, preferred_element_type=F32)


def _dot_nt(a, b, precision=None):
    return lax.dot_general(a, b, (((1,), (1,)), ((), ())), precision=precision,
                           preferred_element_type=F32)


def _dot_tn(a, b):
    return lax.dot_general(a, b, (((0,), (0,)), ((), ())), preferred_element_type=F32)


def _silu(x):
    return x * jax.nn.sigmoid(x)


def _layer_norm(x, g, b):
    mu = jnp.mean(x, -1, keepdims=True)
    xc = x - mu
    var = jnp.mean(xc * xc, -1, keepdims=True)
    return xc * lax.rsqrt(var + LN_EPS) * g + b


def _ada_body(c_ref, w_ref, b_ref, o_ref):
    a = _silu(c_ref[...]).astype(BF16)
    o_ref[...] = _dot(a, w_ref[...].astype(BF16)) + b_ref[...]


def ada_all(c_all, w_ada, b_ada, tn=1024):
    depth, d, n = w_ada.shape
    r = c_all.shape[0]
    return pl.pallas_call(
        _ada_body,
        grid=(depth, n // tn),
        in_specs=[pl.BlockSpec((r, d), lambda l, j: (0, 0)),
                  pl.BlockSpec((None, d, tn), lambda l, j: (l, 0, j)),
                  pl.BlockSpec((None, 1, tn), lambda l, j: (l, 0, j))],
        out_specs=pl.BlockSpec((None, r, tn), lambda l, j: (l, 0, j)),
        out_shape=jax.ShapeDtypeStruct((depth, r, n), F32),
        compiler_params=_params("arbitrary", "arbitrary"),
        name="ada",
    )(c_all, w_ada, b_ada.reshape(depth, 1, n))


def _modmm_body(x_ref, sc_ref, sh_ref, w_ref, o_ref, h_ref):
    @pl.when(pl.program_id(2) == 0)
    def _():
        h = x_ref[...] * (1.0 + sc_ref[...]) + sh_ref[...]
        h_ref[...] = h.reshape(h_ref.shape).astype(BF16)

    o_ref[...] = _dot(h_ref[...], w_ref[...]).reshape(o_ref.shape)


def modulated_matmul(x, ada, sc_chunk, sh_chunk, w, layer, nb, t, tn):
    s, tt, d = x.shape
    n = w.shape[2]
    return pl.pallas_call(
        _modmm_body,
        grid=(s // nb, tt // t, n // tn),
        in_specs=[pl.BlockSpec((nb, t, d), lambda i, r, j: (i, r, 0)),
                  pl.BlockSpec((nb, 1, d), lambda i, r, j: (i, 0, sc_chunk)),
                  pl.BlockSpec((nb, 1, d), lambda i, r, j: (i, 0, sh_chunk)),
                  pl.BlockSpec((None, d, tn), lambda i, r, j: (layer, 0, j))],
        out_specs=pl.BlockSpec((nb, t, tn), lambda i, r, j: (i, r, j)),
        out_shape=jax.ShapeDtypeStruct((s, tt, n), F32),
        scratch_shapes=[pltpu.VMEM((nb * t, d), BF16)],
        compiler_params=_params("arbitrary", "arbitrary", "arbitrary"),
        name="modmm",
    )(x, ada, ada, w)


def _select_blocks(score, n_past, n_blocks, axis):
    blk = lax.broadcasted_iota(jnp.int32, score.shape, axis)
    past = blk < n_past
    s = jnp.where(past, score, -jnp.inf)
    rank = jnp.zeros(score.shape, jnp.int32)
    for m in range(n_blocks):
        sm = s[m:m + 1, :] if axis == 0 else s[:, m:m + 1]
        beats = jnp.where(sm > s, 1, jnp.where(sm == s, jnp.where(blk > m, 1, 0), 0))
        rank = rank + beats
    return jnp.where(past, jnp.where(rank < MOBA_TOPK, 1.0, 0.0), 0.0)


def _moba_prompt_body(q_ref, k_ref, v_ref, o_ref, kb_ref, vt_ref, km_ref, m_scr, l_scr, acc_scr):
    qi = pl.program_id(2)
    n_blocks = k_ref.shape[0] // MOBA_BLOCK
    scale = ATT_DIM ** -0.5
    heads = [slice(g * ATT_DIM, (g + 1) * ATT_DIM) for g in range(MOBA_HEADS_PER_STEP)]

    @pl.when(qi == 0)
    def _():
        for g, hd in enumerate(heads):
            kb_ref[g] = k_ref[:, hd].astype(BF16)
            for n in range(n_blocks):
                rows = slice(n * MOBA_BLOCK, (n + 1) * MOBA_BLOCK)
                vt_ref[g, n] = v_ref[rows, hd].T.astype(BF16)
                km_ref[g, n:n + 1, :] = jnp.mean(k_ref[rows, hd], axis=0, keepdims=True)

    start = pl.multiple_of(qi * MOBA_BLOCK, MOBA_BLOCK)
    key = lax.broadcasted_iota(jnp.int32, (MOBA_BLOCK, MOBA_BLOCK), 0)
    qry = lax.broadcasted_iota(jnp.int32, (MOBA_BLOCK, MOBA_BLOCK), 1)
    qbs, biases = [], []
    for g, hd in enumerate(heads):
        q = q_ref[:, hd]
        qb = q.astype(BF16)
        score = _dot_nt(km_ref[g], q, precision=lax.Precision.HIGHEST)
        biases.append((_select_blocks(score, qi, n_blocks, 0) - 1.0) * (-NEG_BIG))
        qbs.append(qb)
        s0 = _dot_nt(kb_ref[g, pl.ds(start, MOBA_BLOCK), :], qb) * scale
        s0 = jnp.where(key <= qry, s0, NEG_BIG)
        m0 = jnp.max(s0, axis=0, keepdims=True)
        p0 = jnp.exp(s0 - m0)
        m_scr[g] = m0
        l_scr[g] = jnp.sum(p0, axis=0, keepdims=True)
        acc_scr[g] = _dot(vt_ref[g, qi], p0.astype(BF16))

    span = MOBA_CHUNK_BLOCKS * MOBA_BLOCK
    for c in range(n_blocks // MOBA_CHUNK_BLOCKS):
        @pl.when(c * MOBA_CHUNK_BLOCKS < qi)
        def _():
            first = c * MOBA_CHUNK_BLOCKS
            for g in range(MOBA_HEADS_PER_STEP):
                s = _dot_nt(kb_ref[g, c * span:(c + 1) * span, :], qbs[g]) * scale
                s = jnp.concatenate(
                    [s[j * MOBA_BLOCK:(j + 1) * MOBA_BLOCK, :] + biases[g][first + j:first + j + 1, :]
                     for j in range(MOBA_CHUNK_BLOCKS)], axis=0)
                m = m_scr[g]
                m_new = jnp.maximum(m, jnp.max(s, axis=0, keepdims=True))
                alpha = jnp.exp(m - m_new)
                p = jnp.exp(s - m_new)
                l_scr[g] = alpha * l_scr[g] + jnp.sum(p, axis=0, keepdims=True)
                pb = p.astype(BF16)
                pv = _dot(vt_ref[g, first], pb[0:MOBA_BLOCK, :])
                for j in range(1, MOBA_CHUNK_BLOCKS):
                    pv = pv + _dot(vt_ref[g, first + j], pb[j * MOBA_BLOCK:(j + 1) * MOBA_BLOCK, :])
                acc_scr[g] = alpha * acc_scr[g] + pv
                m_scr[g] = m_new

    for g, hd in enumerate(heads):
        o_ref[:, hd] = (acc_scr[g] / l_scr[g]).T.astype(o_ref.dtype)


def moba_prompt(z):
    b, t, _ = z.shape
    assert t % (MOBA_BLOCK * MOBA_CHUNK_BLOCKS) == 0
    nq = t // MOBA_BLOCK
    assert nq % 8 == 0 and ATT_HEADS % MOBA_HEADS_PER_STEP == 0
    hps = MOBA_HEADS_PER_STEP
    w = hps * ATT_DIM
    kcol = OFF_AK // w
    vcol = OFF_AV // w
    return pl.pallas_call(
        _moba_prompt_body,
        grid=(b, ATT_HEADS // hps, nq),
        in_specs=[pl.BlockSpec((None, MOBA_BLOCK, w), lambda i, h, q: (i, q, h)),
                  pl.BlockSpec((None, t, w), lambda i, h, q: (i, 0, kcol + h)),
                  pl.BlockSpec((None, t, w), lambda i, h, q: (i, 0, vcol + h))],
        out_specs=pl.BlockSpec((None, MOBA_BLOCK, w), lambda i, h, q: (i, q, h)),
        out_shape=jax.ShapeDtypeStruct((b, t, ATT_W), BF16),
        scratch_shapes=[pltpu.VMEM((hps, t, ATT_DIM), BF16),
                        pltpu.VMEM((hps, nq, ATT_DIM, MOBA_BLOCK), BF16),
                        pltpu.VMEM((hps, nq, ATT_DIM), F32),
                        pltpu.VMEM((hps, 1, MOBA_BLOCK), F32), pltpu.VMEM((hps, 1, MOBA_BLOCK), F32),
                        pltpu.VMEM((hps, ATT_DIM, MOBA_BLOCK), F32)],
        compiler_params=_params("arbitrary", "arbitrary", "arbitrary"),
        name="moba_prompt",
    )(z, z, z)


def _softmax_partial(qb, kb, vb, bias):
    s = _dot_nt(qb, kb) * (ATT_DIM ** -0.5) + bias
    m = jnp.max(s, axis=1, keepdims=True)
    p = jnp.exp(s - m)
    return m, jnp.sum(p, axis=1, keepdims=True), _dot(p.astype(BF16), vb)


def _moba_sample_body(pt_ref, q_ref, kn_ref, vn_ref, *refs):
    del pt_ref
    n_pages = (len(refs) - 4) // 2
    k_refs, v_refs = refs[:n_pages], refs[n_pages:2 * n_pages]
    o_ref, km_ref, ko_ref, vo_ref = refs[2 * n_pages:]
    pages_per_block = MOBA_BLOCK // PAGE_SIZE
    n_blocks = n_pages // pages_per_block
    rows = q_ref.shape[0]
    hr = ATT_HEADS * rows
    pad_rows = ko_ref.shape[0]

    def head_major(x):
        return jnp.concatenate([x[:, h * ATT_DIM:(h + 1) * ATT_DIM] for h in range(ATT_HEADS)], axis=0)

    q = head_major(q_ref[...])
    qb = q.astype(BF16)

    blk_rows = MOBA_BLOCK * ATT_HEADS
    q_head = lax.broadcasted_iota(jnp.int32, (hr, blk_rows), 0) // rows
    k_head = lax.broadcasted_iota(jnp.int32, (hr, blk_rows), 1) % ATT_HEADS
    head_bias = jnp.where(q_head == k_head, 0.0, NEG_BIG)
    km_ref[...] = jnp.zeros_like(km_ref)
    parts = []
    for n in range(n_blocks):
        pages = range(n * pages_per_block, (n + 1) * pages_per_block)
        kblk = jnp.concatenate([k_refs[pg][...] for pg in pages], axis=0)
        vblk = jnp.concatenate([v_refs[pg][...] for pg in pages], axis=0)
        parts.append(_softmax_partial(qb, kblk.astype(BF16), vblk.astype(BF16), head_bias))
        km_ref[n * ATT_HEADS:(n + 1) * ATT_HEADS, :] = (
            jnp.sum(kblk.reshape(MOBA_BLOCK, ATT_HEADS, ATT_DIM), axis=0) * (1.0 / MOBA_BLOCK))

    sc = _dot_nt(q, km_ref[...], precision=lax.Precision.HIGHEST)
    col = lax.broadcasted_iota(jnp.int32, sc.shape, 1)
    row_head = lax.broadcasted_iota(jnp.int32, sc.shape, 0) // rows
    sc = jnp.where(col % ATT_HEADS == row_head, jnp.where(col < n_blocks * ATT_HEADS, sc, 0.0), 0.0)
    g_row = lax.broadcasted_iota(jnp.int32, (128, 128), 0)
    g_col = lax.broadcasted_iota(jnp.int32, (128, 128), 1)
    gather = jnp.where(g_row // ATT_HEADS == g_col, 1.0, 0.0)
    score = jnp.dot(sc, gather, precision=lax.Precision.HIGHEST, preferred_element_type=F32)
    sel = _select_blocks(score, n_blocks, n_blocks, 1)

    ko_ref[...] = jnp.zeros_like(ko_ref)
    vo_ref[...] = jnp.zeros_like(vo_ref)
    ko_ref[0:hr, :] = head_major(kn_ref[...]).astype(BF16)
    vo_ref[0:hr, :] = head_major(vn_ref[...]).astype(BF16)
    o_qrow = lax.broadcasted_iota(jnp.int32, (hr, pad_rows), 0)
    o_kcol = lax.broadcasted_iota(jnp.int32, (hr, pad_rows), 1)
    own_bias = jnp.where(o_kcol // rows == o_qrow // rows,
                         jnp.where(o_kcol % rows <= o_qrow % rows, 0.0, NEG_BIG), NEG_BIG)
    m_o, l_o, acc_o = _softmax_partial(qb, ko_ref[...], vo_ref[...], own_bias)

    m_tot = m_o
    for n in range(n_blocks):
        m_tot = jnp.maximum(m_tot, jnp.where(sel[:, n:n + 1] > 0.0, parts[n][0], NEG_BIG))
    w_o = jnp.exp(m_o - m_tot)
    num = w_o * acc_o
    den = w_o * l_o
    for n in range(n_blocks):
        m_n, l_n, acc_n = parts[n]
        w = jnp.where(sel[:, n:n + 1] > 0.0, jnp.exp(m_n - m_tot), 0.0)
        num = num + w * acc_n
        den = den + w * l_n
    out = num / den
    for h in range(ATT_HEADS):
        o_ref[:, h * ATT_DIM:(h + 1) * ATT_DIM] = out[h * rows:(h + 1) * rows, :].astype(o_ref.dtype)


def moba_sample(z, cache_k, cache_v, page_table, layer):
    s, rows, _ = z.shape
    n_pages = page_table.shape[1]
    pages_per_block = MOBA_BLOCK // PAGE_SIZE
    assert MOBA_BLOCK % PAGE_SIZE == 0 and n_pages % pages_per_block == 0
    n_blocks = n_pages // pages_per_block
    hr = ATT_HEADS * rows
    assert hr <= 128 and n_blocks * ATT_HEADS <= 128
    depth, n_phys = cache_k.shape[0], cache_k.shape[1]
    ck = cache_k.reshape(depth, n_phys, PAGE_SIZE * ATT_HEADS, ATT_DIM)
    cv = cache_v.reshape(depth, n_phys, PAGE_SIZE * ATT_HEADS, ATT_DIM)

    def page_spec(which):
        return pl.BlockSpec((None, None, PAGE_SIZE * ATT_HEADS, ATT_DIM),
                            lambda i, pt: (layer, pt[i, which], 0, 0))

    def z_spec(col):
        return pl.BlockSpec((None, rows, ATT_W), lambda i, pt: (i, 0, col))

    page_specs = [page_spec(pg) for pg in range(n_pages)]
    grid_spec = pltpu.PrefetchScalarGridSpec(
        num_scalar_prefetch=1,
        grid=(s,),
        in_specs=[z_spec(OFF_AQ // ATT_W), z_spec(OFF_AK // ATT_W), z_spec(OFF_AV // ATT_W)]
        + page_specs + page_specs,
        out_specs=pl.BlockSpec((None, rows, ATT_W), lambda i, pt: (i, 0, 0)),
        scratch_shapes=[pltpu.VMEM((128, ATT_DIM), F32),
                        pltpu.VMEM((128, ATT_DIM), BF16),
                        pltpu.VMEM((128, ATT_DIM), BF16)])
    return pl.pallas_call(
        _moba_sample_body,
        grid_spec=grid_spec,
        out_shape=jax.ShapeDtypeStruct((s, rows, ATT_W), BF16),
        compiler_params=_params("arbitrary"),
        name="moba_sample",
    )(page_table, z, z, z, *([ck] * n_pages), *([cv] * n_pages))


def _retention_body(*refs, rows, has_state, n_prev):
    (q_ref, k_ref, v_ref, g_ref, cos_ref, sin_ref, intra_ref, qd_ref, kd_ref, cd_ref, gn_ref) = refs[:11]
    rest = list(refs[11:])
    r0_ref = rest.pop(0) if has_state else None
    prev_ref = rest.pop(0) if n_prev else None
    o_ref, rout_ref, r_scr = rest
    ci = pl.program_id(1)
    c = intra_ref.shape[1]

    @pl.when(ci == 0)
    def _():
        if has_state:
            r_scr[...] = r0_ref[...]
        else:
            r_scr[...] = jnp.zeros_like(r_scr)

    def pad(x):
        if rows == c:
            return x
        return jnp.concatenate([x, jnp.zeros((c - rows, x.shape[1]), x.dtype)], axis=0)

    cos = pad(cos_ref[...])
    sin = pad(sin_ref[...])
    even = lax.broadcasted_iota(jnp.int32, cos.shape, 1) % 2 == 0

    def rotate(x):
        nxt = pltpu.roll(x, x.shape[1] - 1, axis=1)
        prv = pltpu.roll(x, 1, axis=1)
        return x * cos + jnp.where(even, nxt, prv) * sin

    for h in range(RET_HEADS):
        qk = slice(h * RET_DK, (h + 1) * RET_DK)
        vv = slice(h * RET_DV, (h + 1) * RET_DV)
        q = rotate(pad(q_ref[:, qk]))
        k = rotate(pad(k_ref[:, qk])) * (RET_DK ** -0.5)
        v = pad(v_ref[:, vv])
        r = r_scr[h]
        qb, kb, vb = q.astype(BF16), k.astype(BF16), v.astype(BF16)
        s = _dot_nt(qb, kb) * intra_ref[h]
        o = _dot(s.astype(BF16), vb) + _dot(qb, r.astype(BF16)) * qd_ref[h]
        r_new = r * cd_ref[h] + _dot_tn((k * kd_ref[h]).astype(BF16), vb)
        r_scr[h] = r_new

        mu = jnp.mean(o, -1, keepdims=True)
        oc = o - mu
        var = jnp.mean(oc * oc, -1, keepdims=True)
        on = oc * lax.rsqrt(var + LN_EPS) * gn_ref[:, vv]
        y = _silu(pad(g_ref[:, vv])) * on
        o_ref[:, vv] = y[0:rows, :].astype(o_ref.dtype)

    @pl.when(ci == pl.num_programs(1) - 1)
    def _():
        if n_prev:
            rout_ref[0:n_prev] = prev_ref[...]
        rout_ref[n_prev] = r_scr[...]


def _retention_tables(t, pos0, c_true, c_pad):
    inv = 1.0 / (10000.0 ** jnp.linspace(0.0, 1.0, RET_DK // 2, dtype=F32))
    pos = pos0 + jnp.arange(t, dtype=jnp.int32)
    ang = pos.astype(F32)[:, None] * inv[None, :]
    cos = jnp.repeat(jnp.cos(ang), 2, axis=-1)
    sin = jnp.stack([-jnp.sin(ang), jnp.sin(ang)], axis=-1).reshape(t, RET_DK)
    log_g = jnp.log1p(-jnp.exp2(-5.0 - jnp.arange(RET_HEADS, dtype=F32)))
    i = jnp.arange(c_pad, dtype=F32)
    live = i < c_true
    diff = i[:, None] - i[None, :]
    intra = jnp.where(diff >= 0, jnp.exp(log_g[:, None, None] * jnp.maximum(diff, 0.0)), 0.0)
    intra = jnp.where(live[None, :, None] & live[None, None, :], intra, 0.0)
    q_decay = jnp.exp(log_g[:, None] * (i[None, :] + 1.0))
    k_decay = jnp.where(live[None, :], jnp.exp(log_g[:, None] * (c_true - 1.0 - i[None, :])), 0.0)
    chunk_decay = jnp.exp(log_g * c_true)
    return (cos, sin, intra, q_decay[:, :, None], k_decay[:, :, None], chunk_decay[:, None, None])


def retention(z, gn_g, state, layer, pos0, prev_new):
    s, t, _ = z.shape
    c_true = math.gcd(RET_CHUNK, t)
    c_pad = RET_CHUNK
    n_chunks = t // c_true
    cos, sin, intra, qd, kd, cd = _retention_tables(t, pos0, c_true, c_pad)
    has_state = state is not None
    qk_w, v_w = RET_HEADS * RET_DK, RET_HEADS * RET_DV

    def zspec(off, width):
        return pl.BlockSpec((None, c_true, width), lambda i, ci: (i, ci, off // width))

    whole = lambda a: pl.BlockSpec(a.shape, lambda i, ci: (0,) * a.ndim)
    n_prev = 0 if prev_new is None else prev_new.shape[0]

    def stacked_spec(n):
        return pl.BlockSpec((n, None, RET_HEADS, RET_DK, RET_DV), lambda i, ci: (0, i, 0, 0, 0))

    gn = gn_g.reshape(1, -1)
    in_specs = [zspec(OFF_BQ, qk_w), zspec(OFF_BK, qk_w), zspec(OFF_BV, v_w), zspec(OFF_BG, v_w),
                pl.BlockSpec((c_true, RET_DK), lambda i, ci: (ci, 0)),
                pl.BlockSpec((c_true, RET_DK), lambda i, ci: (ci, 0)),
                whole(intra), whole(qd), whole(kd), whole(cd), whole(gn)]
    args = [z, z, z, z, cos, sin, intra, qd, kd, cd, gn]
    if has_state:
        in_specs.append(pl.BlockSpec((None, None, RET_HEADS, RET_DK, RET_DV),
                                     lambda i, ci: (layer, i, 0, 0, 0)))
        args.append(state)
    if n_prev:
        in_specs.append(stacked_spec(n_prev))
        args.append(prev_new)
    return pl.pallas_call(
        functools.partial(_retention_body, rows=c_true, has_state=has_state, n_prev=n_prev),
        grid=(s, n_chunks),
        in_specs=in_specs,
        out_specs=[pl.BlockSpec((None, c_true, v_w), lambda i, ci: (i, ci, 0)), stacked_spec(n_prev + 1)],
        out_shape=[jax.ShapeDtypeStruct((s, t, v_w), BF16),
                   jax.ShapeDtypeStruct((n_prev + 1, s, RET_HEADS, RET_DK, RET_DV), F32)],
        scratch_shapes=[pltpu.VMEM((RET_HEADS, RET_DK, RET_DV), F32)],
        compiler_params=_params("arbitrary", "arbitrary"),
        name="retention",
    )(*args)


def _conformer_body(*refs, has_state):
    if has_state:
        a_ref, g_ref, w_ref, b_ref, lg_ref, lb_ref, st_ref, o_ref, cn_ref, buf = refs
    else:
        a_ref, g_ref, w_ref, b_ref, lg_ref, lb_ref, o_ref, cn_ref, buf = refs
    ti = pl.program_id(1)
    nb, t, ch = a_ref.shape
    keep = CONV_WIDTH - 1
    lead = HIST_ROWS - keep

    @pl.when(ti == 0)
    def _():
        buf[:, 0:HIST_ROWS, :] = jnp.zeros((nb, HIST_ROWS, ch), F32)
        if has_state:
            buf[:, lead:HIST_ROWS, :] = st_ref[...]

    buf[:, HIST_ROWS:HIST_ROWS + t, :] = a_ref[...] * jax.nn.sigmoid(g_ref[...])
    acc = jnp.zeros((nb, t, ch), F32) + b_ref[...]
    for shift in range(8):
        taps = [j for j in range(CONV_WIDTH) if (lead + j) % 8 == shift]
        span = t + (8 if shift else 0)
        part = None
        for j in taps:
            base = lead + j - shift
            term = w_ref[j:j + 1, :] * buf[:, base:base + span, :]
            part = term if part is None else part + term
        if part is not None:
            acc = acc + part[:, shift:shift + t, :]
    y = _silu(_layer_norm(acc, lg_ref[...], lb_ref[...]))
    o_ref[...] = y.astype(o_ref.dtype)

    hist = buf[:, t:t + HIST_ROWS, :]
    buf[:, 0:HIST_ROWS, :] = hist

    @pl.when(ti == pl.num_programs(1) - 1)
    def _():
        cn_ref[...] = hist[:, lead:HIST_ROWS, :]


def conformer_conv(z, conv_w, conv_b, ln_g, ln_b, state, layer, nb, t):
    s, tt, _ = z.shape
    keep = CONV_WIDTH - 1
    has_state = state is not None
    acol = OFF_C // CONV_CH
    in_specs = [pl.BlockSpec((nb, t, CONV_CH), lambda i, r: (i, r, acol)),
                pl.BlockSpec((nb, t, CONV_CH), lambda i, r: (i, r, acol + 1)),
                pl.BlockSpec((CONV_WIDTH, CONV_CH), lambda i, r: (0, 0)),
                pl.BlockSpec((1, CONV_CH), lambda i, r: (0, 0)),
                pl.BlockSpec((1, CONV_CH), lambda i, r: (0, 0)),
                pl.BlockSpec((1, CONV_CH), lambda i, r: (0, 0))]
    args = [z, z, conv_w, conv_b.reshape(1, -1), ln_g.reshape(1, -1), ln_b.reshape(1, -1)]
    if has_state:
        in_specs.append(pl.BlockSpec((None, nb, keep, CONV_CH), lambda i, r: (layer, i, 0, 0)))
        args.append(state)
    return pl.pallas_call(
        functools.partial(_conformer_body, has_state=has_state),
        grid=(s // nb, tt // t),
        in_specs=in_specs,
        out_specs=[pl.BlockSpec((nb, t, CONV_CH), lambda i, r: (i, r, 0)),
                   pl.BlockSpec((nb, keep, CONV_CH), lambda i, r: (i, 0, 0))],
        out_shape=[jax.ShapeDtypeStruct((s, tt, CONV_CH), BF16),
                   jax.ShapeDtypeStruct((s, keep, CONV_CH), F32)],
        scratch_shapes=[pltpu.VMEM((nb, HIST_ROWS + t, CONV_CH), F32)],
        compiler_params=_params("arbitrary", "arbitrary"),
        name="conformer_conv",
    )(*args)


def _merge_body(ya_ref, yb_ref, yc_ref, w_ref, *refs):
    o_ref = refs[-1]
    g_refs = refs[:-1]
    nb, t, d = o_ref.shape
    parts = len(g_refs) // 3
    ys = [y_ref[...].reshape(nb * t, y_ref.shape[2]) for y_ref in (ya_ref, yb_ref, yc_ref)]
    for part in range(parts):
        cols = slice(part * MERGE_GATE_COLS, (part + 1) * MERGE_GATE_COLS)
        total = None
        for i in range(3):
            term = (jax.nn.sigmoid(g_refs[i * parts + part][...])
                    * _dot(ys[i], w_ref[i, :, cols]).reshape(nb, t, MERGE_GATE_COLS))
            total = term if total is None else total + term
        o_ref[:, :, cols] = total.astype(o_ref.dtype)


def gated_merge(ya, yb, yc, z, w_branch, layer, nb, t):
    s, tt, bw = ya.shape
    d = w_branch.shape[3]
    assert d % MERGE_GATE_COLS == 0 and OFF_GA % MERGE_GATE_COLS == 0
    parts = d // MERGE_GATE_COLS
    gcol = OFF_GA // MERGE_GATE_COLS
    yspec = pl.BlockSpec((nb, t, bw), lambda i, r: (i, r, 0))

    def gspec(col):
        return pl.BlockSpec((nb, t, MERGE_GATE_COLS), lambda i, r: (i, r, gcol + col))

    return pl.pallas_call(
        _merge_body,
        grid=(s // nb, tt // t),
        in_specs=[yspec, yspec, yspec,
                  pl.BlockSpec((None, 3, bw, d), lambda i, r: (layer, 0, 0, 0))]
        + [gspec(col) for col in range(3 * parts)],
        out_specs=pl.BlockSpec((nb, t, d), lambda i, r: (i, r, 0)),
        out_shape=jax.ShapeDtypeStruct((s, tt, d), BF16),
        compiler_params=_params("arbitrary", "arbitrary"),
        name="gated_merge",
    )(ya, yb, yc, w_branch, *([z] * (3 * parts)))


def _outproj_body(x_ref, m_ref, gate_ref, w_ref, lg_ref, lb_ref, o_ref, *, alpha):
    nb, t, d = x_ref.shape
    y = _dot(m_ref[...].reshape(nb * t, m_ref.shape[2]), w_ref[...]).reshape(nb, t, d)
    o_ref[...] = _layer_norm(alpha * x_ref[...] + (1.0 + gate_ref[...]) * y, lg_ref[...], lb_ref[...])


def outproj_norm(x, merged, ada, gate_chunk, w_o, layer, ln_g, ln_b, alpha, nb, t):
    s, tt, d = x.shape
    tok = lambda i, r: (i, r, 0)
    return pl.pallas_call(
        functools.partial(_outproj_body, alpha=alpha),
        grid=(s // nb, tt // t),
        in_specs=[pl.BlockSpec((nb, t, d), tok),
                  pl.BlockSpec((nb, t, merged.shape[2]), tok),
                  pl.BlockSpec((nb, 1, d), lambda i, r: (i, 0, gate_chunk)),
                  pl.BlockSpec((None,) + w_o.shape[1:], lambda i, r: (layer, 0, 0)),
                  pl.BlockSpec((1, d), lambda i, r: (0, 0)),
                  pl.BlockSpec((1, d), lambda i, r: (0, 0))],
        out_specs=pl.BlockSpec((nb, t, d), tok),
        out_shape=jax.ShapeDtypeStruct((s, tt, d), F32),
        compiler_params=_params("arbitrary", "arbitrary"),
        name="outproj_norm",
    )(x, merged, ada, w_o, ln_g.reshape(1, -1), ln_b.reshape(1, -1))


def _ffn_up_body(*refs, has_state):
    if has_state:
        (x_ref, sc_ref, sh_ref, wg_ref, wu_ref, cw_ref, cb_ref, st_ref,
         act_ref, tail_ref, h_ref, gbuf, carry) = refs
    else:
        (x_ref, sc_ref, sh_ref, wg_ref, wu_ref, cw_ref, cb_ref,
         act_ref, tail_ref, h_ref, gbuf, carry) = refs
    r = pl.program_id(1)
    j = pl.program_id(2)
    nb, t, tn = act_ref.shape
    keep = FFN_CONV_WIDTH - 1
    lead = FFN_HIST_ROWS - keep

    @pl.when(j == 0)
    def _():
        h = x_ref[...] * (1.0 + sc_ref[...]) + sh_ref[...]
        h_ref[...] = h.reshape(h_ref.shape).astype(BF16)

    h = h_ref[...]
    g = _dot(h, wg_ref[...]).reshape(nb, t, tn)
    u = _dot(h, wu_ref[...]).reshape(nb, t, tn)
    if has_state:
        prev = st_ref[...]
    else:
        prev = jnp.where(r == 0, 0.0, carry[j][:, lead:FFN_HIST_ROWS, :])
    gbuf[:, lead:FFN_HIST_ROWS, :] = prev
    gbuf[:, FFN_HIST_ROWS:FFN_HIST_ROWS + t, :] = g
    conv = jnp.zeros((nb, t, tn), F32) + cb_ref[...]
    for tap in range(FFN_CONV_WIDTH):
        conv = conv + cw_ref[tap:tap + 1, :] * gbuf[:, lead + tap:lead + tap + t, :]
    act_ref[...] = (_silu(conv) * u).astype(act_ref.dtype)
    tail = g[:, t - FFN_HIST_ROWS:t, :]
    tail_ref[...] = tail
    if not has_state:
        carry[j] = tail


def ffn_up_act(x, ada, sc_chunk, sh_chunk, w_up, conv_w, conv_b, state, layer, nb, t, tn):
    s, tt, d = x.shape
    f = w_up.shape[2] // 2
    nk = f // tn
    has_state = state is not None
    assert t >= FFN_HIST_ROWS and (not has_state or tt == t) and (has_state or nb == 1)
    in_specs = [pl.BlockSpec((nb, t, d), lambda i, r, j: (i, r, 0)),
                pl.BlockSpec((nb, 1, d), lambda i, r, j: (i, 0, sc_chunk)),
                pl.BlockSpec((nb, 1, d), lambda i, r, j: (i, 0, sh_chunk)),
                pl.BlockSpec((None, d, tn), lambda i, r, j: (layer, 0, j)),
                pl.BlockSpec((None, d, tn), lambda i, r, j: (layer, 0, nk + j)),
                pl.BlockSpec((FFN_CONV_WIDTH, tn), lambda i, r, j: (0, j)),
                pl.BlockSpec((1, tn), lambda i, r, j: (0, j))]
    args = [x, ada, ada, w_up, w_up, conv_w, conv_b.reshape(1, -1)]
    if has_state:
        in_specs.append(pl.BlockSpec((None, nb, FFN_CONV_WIDTH - 1, tn), lambda i, r, j: (layer, i, 0, j)))
        args.append(state)
    return pl.pallas_call(
        functools.partial(_ffn_up_body, has_state=has_state),
        grid=(s // nb, tt // t, nk),
        in_specs=in_specs,
        out_specs=[pl.BlockSpec((nb, t, tn), lambda i, r, j: (i, r, j)),
                   pl.BlockSpec((nb, FFN_HIST_ROWS, tn), lambda i, r, j: (i, r, j))],
        out_shape=[jax.ShapeDtypeStruct((s, tt, f), BF16),
                   jax.ShapeDtypeStruct((s, (tt // t) * FFN_HIST_ROWS, f), F32)],
        scratch_shapes=[pltpu.VMEM((nb * t, d), BF16),
                        pltpu.VMEM((nb, FFN_HIST_ROWS + t, tn), F32),
                        pltpu.VMEM((nk, nb, FFN_HIST_ROWS, tn), F32)],
        compiler_params=_params("arbitrary", "arbitrary", "arbitrary"),
        name="ffn_up_act",
    )(*args)


def _ffn_down_body(a_ref, wd_ref, x_ref, gate_ref, lg_ref, lb_ref, o_ref, acc, *, alpha):
    kk = pl.program_id(2)
    nb, t, tk = a_ref.shape

    @pl.when(kk == 0)
    def _():
        acc[...] = jnp.zeros_like(acc)

    acc[...] += _dot(a_ref[...].reshape(nb * t, tk), wd_ref[...])

    @pl.when(kk == pl.num_programs(2) - 1)
    def _():
        y = acc[...].reshape(o_ref.shape)
        o_ref[...] = _layer_norm(alpha * x_ref[...] + (1.0 + gate_ref[...]) * y, lg_ref[...], lb_ref[...])


def ffn_down_norm(act, x, ada, gate_chunk, w_down, layer, ln_g, ln_b, alpha, nb, t, tk):
    s, tt, f = act.shape
    d = x.shape[2]
    assert f % tk == 0
    tok = lambda i, r, k: (i, r, 0)
    return pl.pallas_call(
        functools.partial(_ffn_down_body, alpha=alpha),
        grid=(s // nb, tt // t, f // tk),
        in_specs=[pl.BlockSpec((nb, t, tk), lambda i, r, k: (i, r, k)),
                  pl.BlockSpec((None, tk, d), lambda i, r, k: (layer, k, 0)),
                  pl.BlockSpec((nb, t, d), tok),
                  pl.BlockSpec((nb, 1, d), lambda i, r, k: (i, 0, gate_chunk)),
                  pl.BlockSpec((1, d), lambda i, r, k: (0, 0)),
                  pl.BlockSpec((1, d), lambda i, r, k: (0, 0))],
        out_specs=pl.BlockSpec((nb, t, d), tok),
        out_shape=jax.ShapeDtypeStruct((s, tt, d), F32),
        scratch_shapes=[pltpu.VMEM((nb * t, d), F32)],
        compiler_params=_params("arbitrary", "arbitrary", "arbitrary"),
        name="ffn_down_norm",
    )(act, w_down, x, ada, ln_g.reshape(1, -1), ln_b.reshape(1, -1))


def _kv_rows_body(k_ref, v_ref, ko_ref, vo_ref):
    t = k_ref.shape[1]
    for h in range(ATT_HEADS):
        cols = slice(h * ATT_DIM, (h + 1) * ATT_DIM)
        ko_ref[:, pl.ds(h, t, stride=ATT_HEADS), :] = k_ref[:, :, cols]
        vo_ref[:, pl.ds(h, t, stride=ATT_HEADS), :] = v_ref[:, :, cols]


def kv_rows(z, nb, t):
    s, tt, _ = z.shape
    out_spec = pl.BlockSpec((nb, t * ATT_HEADS, ATT_DIM), lambda i, r: (i, r, 0))
    out_shape = jax.ShapeDtypeStruct((s, tt * ATT_HEADS, ATT_DIM), F32)
    k, v = pl.pallas_call(
        _kv_rows_body,
        grid=(s // nb, tt // t),
        in_specs=[pl.BlockSpec((nb, t, ATT_W), lambda i, r: (i, r, OFF_AK // ATT_W)),
                  pl.BlockSpec((nb, t, ATT_W), lambda i, r: (i, r, OFF_AV // ATT_W))],
        out_specs=[out_spec, out_spec],
        out_shape=[out_shape, out_shape],
        compiler_params=_params("arbitrary", "arbitrary"),
        name="kv_rows",
    )(z, z)
    return k.reshape(s, tt, ATT_HEADS, ATT_DIM), v.reshape(s, tt, ATT_HEADS, ATT_DIM)


def _tile(s, t, rows):
    if t >= rows:
        assert t % rows == 0
        return 1, rows
    nb = max(1, min(s, rows // t))
    assert s % nb == 0
    return nb, t


def trunk_layer(x, ada, attend, conv_state, ret_state, ffn_state, layer, pos0, alpha, p, ret_prev):
    s, t, d = x.shape
    nb, tr = _tile(s, t, 1024 if t >= 1024 else 512)
    z = modulated_matmul(x, ada, 1, 0, p["w_in"], layer, nb, tr, 512)
    ya = attend(z)
    yb, ret_new = retention(z, p["ret_gn_g"], ret_state, layer, pos0, ret_prev)
    nbc, trc = _tile(s, t, 256) if t >= 256 else _tile(s, t, 128)
    yc, conv_new = conformer_conv(z, p["conv_w"], p["conv_b"], p["conv_ln_g"], p["conv_ln_b"],
                                  conv_state, layer, nbc, trc)
    nbm, trm = _tile(s, t, 512)
    nbg, trg = _tile(s, t, 256)
    merged = gated_merge(ya, yb, yc, z, p["w_branch"], layer, nbg, trg)
    x1 = outproj_norm(x, merged, ada, 2, p["w_o"], layer, p["ln1_g"], p["ln1_b"], alpha, nbm, trm)
    act, g_tail = ffn_up_act(x1, ada, 4, 3, p["w_up"], p["ffn_conv_w"], p["ffn_conv_b"], ffn_state, layer,
                             nb, tr, 512)
    x2 = ffn_down_norm(act, x1, ada, 5, p["w_down"], layer, p["ln2_g"], p["ln2_b"], alpha, nbm, trm,
                       FFN_DOWN_CHUNK)
    k_new, v_new = kv_rows(z, nbm, trm)
    ffn_new = g_tail[:, g_tail.shape[1] - (FFN_CONV_WIDTH - 1):, :]
    return x2, k_new, v_new, conv_new, ret_new, ffn_new


def kernel(x_prompt, x_sample, cache_k, cache_v, state_conv, state_ret, state_ffn, page_table,
           c_prompt, c_sample, w_ada, b_ada, w_in, w_branch, w_o, ln1_g, ln1_b, ret_gn_g,
           conv_w, conv_b, conv_ln_g, conv_ln_b, w_up, ffn_conv_w, ffn_conv_b, w_down, ln2_g, ln2_b):
    depth = w_ada.shape[0]
    b = x_prompt.shape[0]
    sb = x_sample.shape[0]
    alpha = (2 * depth) ** 0.25
    past = page_table.shape[1] * PAGE_SIZE

    c_all = jnp.concatenate([c_prompt, c_sample], axis=0)
    pad = (-c_all.shape[0]) % 8
    ada = ada_all(jnp.pad(c_all, ((0, pad), (0, 0))), w_ada, b_ada)
    ada_p = ada[:, :b, None, :]
    ada_s = ada[:, b:b + sb, None, :]

    wb = {"w_in": w_in.astype(BF16), "w_branch": w_branch.astype(BF16), "w_o": w_o.astype(BF16),
          "w_up": w_up.astype(BF16), "w_down": w_down.astype(BF16)}
    xp, xs = x_prompt, x_sample
    outs_p, outs_s = [], []
    RET = 3
    for l in range(depth):
        p = {**wb,
             "ln1_g": ln1_g[l], "ln1_b": ln1_b[l], "ret_gn_g": ret_gn_g[l], "conv_w": conv_w[l],
             "conv_b": conv_b[l], "conv_ln_g": conv_ln_g[l], "conv_ln_b": conv_ln_b[l],
             "ffn_conv_w": ffn_conv_w[l], "ffn_conv_b": ffn_conv_b[l], "ln2_g": ln2_g[l], "ln2_b": ln2_b[l]}
        xp, *rest_p = trunk_layer(xp, ada_p[l], moba_prompt, None, None, None, l, 0, alpha, p,
                                  outs_p[-1][RET] if outs_p else None)
        attend_s = functools.partial(moba_sample, cache_k=cache_k, cache_v=cache_v,
                                     page_table=page_table, layer=l)
        xs, *rest_s = trunk_layer(xs, ada_s[l], attend_s, state_conv, state_ret, state_ffn, l,
                                  past, alpha, p, outs_s[-1][RET] if outs_s else None)
        outs_p.append(rest_p)
        outs_s.append(rest_s)

    def stack(outs, i):
        return outs[-1][i] if i == RET else jnp.stack([o[i] for o in outs])

    return (xp, xs, *(stack(outs_p, i) for i in range(5)), *(stack(outs_s, i) for i in range(5)))
```
